```python
import math
import jax, jax.numpy as jnp
from jax import lax
import numpy as np

D_MODEL = 4096
BATCH = 2
SEQ = 4096
DEPTH = 2

CHUNK = 64
N_META = 16
Q_BLOCK = 128
EPS = 1e-6

MLA_HEADS = 16
MLA_Q_LORA = 1024
MLA_KV_LORA = 512
MLA_NOPE = 128
MLA_ROPE = 64
MLA_V = 128
MLA_QK = MLA_NOPE + MLA_ROPE
MLA_WIDTH = MLA_HEADS * MLA_V
ROPE_THETA = 10000.0

S5_WIDTH = 1024
S5_GROUP = 16
S5_GROUPS = S5_WIDTH // S5_GROUP
S5_STATE = 64
S5_DT_MIN = 1e-3
S5_DT_MAX = 1e-1

DN_HEADS = 8
DN_DK = 128
DN_DV = 128
DN_QK = DN_HEADS * DN_DK
DN_WIDTH = DN_HEADS * DN_DV
DN_CONV = 4
DN_CHUNK = CHUNK

FFN_HIDDEN = 11008

N_BRANCH = 3

OFF_Q = 0
OFF_KV = OFF_Q + MLA_Q_LORA
OFF_KR = OFF_KV + MLA_KV_LORA
OFF_S5 = OFF_KR + MLA_ROPE
OFF_DN_QKV = OFF_S5 + S5_WIDTH
OFF_DN_Z = OFF_DN_QKV + 2 * DN_QK + DN_WIDTH
OFF_DN_A = OFF_DN_Z + DN_WIDTH
OFF_DN_B = OFF_DN_A + DN_HEADS
OFF_GATE = OFF_DN_B + DN_HEADS
N_IN = OFF_GATE + N_BRANCH * D_MODEL

kernel_name = 'hybrid_mla_s5_gdn_macaron_encoder'


def rms_norm(x, g):
    xf = x.astype(jnp.float32)
    y = xf * lax.rsqrt(jnp.mean(xf * xf, axis=-1, keepdims=True) + EPS)
    return (y * g.astype(jnp.float32)).astype(x.dtype)


def l2_norm(x):
    return x * lax.rsqrt(jnp.sum(x * x, axis=-1, keepdims=True) + EPS)


def swiglu(x, w13, w2):
    a, b = jnp.split(x @ w13, 2, axis=-1)
    return (jax.nn.silu(a) * b) @ w2


def chunk_ids(n):
    p = jnp.arange(n)
    return jnp.where(p < N_META, 0, 1 + (p - N_META) // CHUNK)


def rope_tables(n):
    inv = ROPE_THETA ** (-jnp.arange(0, MLA_ROPE, 2, dtype=jnp.float32) / MLA_ROPE)
    ang = jnp.arange(n, dtype=jnp.float32)[:, None] * inv[None, :]
    return jnp.cos(ang), jnp.sin(ang)


def apply_rope(x, cos, sin):
    x1, x2 = jnp.split(x.astype(jnp.float32), 2, axis=-1)
    return jnp.concatenate([x1 * cos - x2 * sin, x2 * cos + x1 * sin], axis=-1).astype(x.dtype)


def mla_branch(c_q, c_kv, k_rope_raw, q_norm_g, kv_norm_g, w_uq, w_ukv, w_o, cos, sin, cid):
    B, L, _ = c_q.shape
    dt = c_q.dtype
    q = (rms_norm(c_q, q_norm_g) @ w_uq).reshape(B, L, MLA_HEADS, MLA_QK)
    q_nope = q[..., :MLA_NOPE]
    q_rope = apply_rope(q[..., MLA_NOPE:], cos[:, None, :], sin[:, None, :])
    kv = (rms_norm(c_kv, kv_norm_g) @ w_ukv).reshape(B, L, MLA_HEADS, MLA_NOPE + MLA_V)
    k_nope, v = kv[..., :MLA_NOPE], kv[..., MLA_NOPE:]
    k_rope = apply_rope(k_rope_raw, cos, sin)
    scale = MLA_QK ** -0.5
    n_blk = -(-L // Q_BLOCK)
    pad = n_blk * Q_BLOCK - L

    def blocks(t):
        t = jnp.pad(t, [(0, 0), (0, pad)] + [(0, 0)] * (t.ndim - 2))
        return jnp.moveaxis(t.reshape((B, n_blk, Q_BLOCK) + t.shape[2:]), 1, 0)

    q_cid = jnp.pad(cid, (0, pad), mode='edge').reshape(n_blk, Q_BLOCK)

    def attend(args):
        qn, qr, qc = args
        s = (jnp.einsum('bqhd,bkhd->bhqk', qn, k_nope, preferred_element_type=jnp.float32)
             + jnp.einsum('bqhr,bkr->bhqk', qr, k_rope, preferred_element_type=jnp.float32)) * scale
        visible = qc[:, None] >= cid[None, :]
        p = jax.nn.softmax(jnp.where(visible, s, -jnp.inf), axis=-1).astype(dt)
        return jnp.einsum('bhqk,bkhd->bqhd', p, v)

    o = lax.map(attend, (blocks(q_nope), blocks(q_rope), q_cid))
    o = jnp.moveaxis(o, 0, 1).reshape(B, n_blk * Q_BLOCK, MLA_WIDTH)[:, :L]
    return o @ w_o


def s5_branch(u, a_re, a_im, log_dt, b_re, b_im, c_re, c_im, d, w_glu):
    B, L, _ = u.shape
    dt = u.dtype
    f32 = jnp.float32
    uf = u.astype(f32).reshape(B, L, S5_GROUPS, S5_GROUP)
    ar, ai = a_re.astype(f32), a_im.astype(f32)
    delta = jnp.exp(log_dt.astype(f32))[:, None]
    mag = jnp.exp(ar * delta)
    abar_r, abar_i = mag * jnp.cos(ai * delta), mag * jnp.sin(ai * delta)
    den = ar * ar + ai * ai
    zr = ((abar_r - 1.0) * ar + abar_i * ai) / den
    zi = (abar_i * ar - (abar_r - 1.0) * ai) / den
    br, bi = b_re.astype(f32), b_im.astype(f32)
    bbar_r = zr[..., None] * br - zi[..., None] * bi
    bbar_i = zr[..., None] * bi + zi[..., None] * br
    bu_r = jnp.einsum('blgc,gpc->lbgp', uf, bbar_r)
    bu_i = jnp.einsum('blgc,gpc->lbgp', uf, bbar_i)
    a_r = jnp.broadcast_to(abar_r, bu_r.shape)
    a_i = jnp.broadcast_to(abar_i, bu_i.shape)

    def combine(e1, e2):
        a1r, a1i, b1r, b1i = e1
        a2r, a2i, b2r, b2i = e2
        return (a2r * a1r - a2i * a1i, a2r * a1i + a2i * a1r,
                a2r * b1r - a2i * b1i + b2r, a2r * b1i + a2i * b1r + b2i)

    _, _, xr, xi = lax.associative_scan(combine, (a_r, a_i, bu_r, bu_i), axis=0)
    y = (jnp.einsum('lbgp,gcp->blgc', xr, c_re.astype(f32))
         - jnp.einsum('lbgp,gcp->blgc', xi, c_im.astype(f32))
         + d.astype(f32) * uf)
    h = jax.nn.gelu(y.reshape(B, L, S5_WIDTH)).astype(dt)
    val, gate = jnp.split(h @ w_glu, 2, axis=-1)
    return val * jax.nn.sigmoid(gate)


def causal_dwconv(x, w):
    k = w.shape[0]
    return lax.conv_general_dilated(x, w[:, None, :], window_strides=(1,), padding=[(k - 1, 0)],
                                    dimension_numbers=('NWC', 'WIO', 'NWC'),
                                    feature_group_count=x.shape[-1])


def gated_deltanet_branch(qkv, z, a, b, conv_w, a_log, dt_bias, out_norm_g, w_o):
    B, L, _ = qkv.shape
    dt = qkv.dtype
    f32 = jnp.float32
    C = DN_CHUNK
    qkv = jax.nn.silu(causal_dwconv(qkv, conv_w))
    q, k, v = jnp.split(qkv.astype(f32), [DN_QK, 2 * DN_QK], axis=-1)
    q = l2_norm(q.reshape(B, L, DN_HEADS, DN_DK)) * (DN_DK ** -0.5)
    k = l2_norm(k.reshape(B, L, DN_HEADS, DN_DK))
    v = v.reshape(B, L, DN_HEADS, DN_DV)
    beta = jax.nn.sigmoid(b.astype(f32))
    g = -jnp.exp(a_log.astype(f32)) * jax.nn.softplus(a.astype(f32) + dt_bias.astype(f32))
    n_c = -(-L // C)
    pad = n_c * C - L

    def chunks(t):
        t = jnp.pad(t, [(0, 0), (0, pad)] + [(0, 0)] * (t.ndim - 2))
        t = t.reshape((B, n_c, C) + t.shape[2:])
        return jnp.moveaxis(t, (1, 3), (0, 2))

    qc, kc, vc, bc, gc = chunks(q), chunks(k), chunks(v), chunks(beta), chunks(g)
    gcum = jnp.cumsum(gc, axis=-1)
    tri = jnp.tril(jnp.ones((C, C), dtype=bool))
    strict = jnp.tril(jnp.ones((C, C), dtype=bool), -1)
    decay = jnp.exp(jnp.where(tri, gcum[..., :, None] - gcum[..., None, :], -jnp.inf))
    k_beta = kc * bc[..., None]
    v_beta = vc * bc[..., None]
    a_mat = jnp.where(strict, jnp.einsum('...id,...jd->...ij', k_beta, kc) * decay, 0.0)

    def solve(rhs):
        return lax.linalg.triangular_solve(a_mat, rhs, left_side=True, lower=True, unit_diagonal=True)

    u_c = solve(v_beta)
    w_c = solve(k_beta * jnp.exp(gcum)[..., None])
    qk = jnp.einsum('...id,...jd->...ij', qc, kc) * decay

    def step(S, inp):
        q_i, k_i, u_i, w_i, qk_i, g_i = inp
        v_new = u_i - jnp.einsum('bhck,bhkv->bhcv', w_i, S)
        o = (jnp.einsum('bhck,bhkv->bhcv', q_i * jnp.exp(g_i)[..., None], S)
             + jnp.einsum('bhij,bhjv->bhiv', qk_i, v_new))
        g_last = g_i[..., -1:]
        S = (S * jnp.exp(g_last)[..., None]
             + jnp.einsum('bhck,bhcv->bhkv', k_i * jnp.exp(g_last - g_i)[..., None], v_new))
        return S, o

    S0 = jnp.zeros((B, DN_HEADS, DN_DK, DN_DV), f32)
    _, o = lax.scan(step, S0, (qc, kc, u_c, w_c, qk, gcum))
    o = jnp.moveaxis(o, (0, 2), (1, 3)).reshape(B, n_c * C, DN_HEADS, DN_DV)[:, :L]
    zf = z.astype(f32).reshape(B, L, DN_HEADS, DN_DV)
    o = rms_norm(o, out_norm_g) * jax.nn.silu(zf)
    return o.reshape(B, L, DN_WIDTH).astype(dt) @ w_o


def hybrid_mixer(h, w_in, mla_q_norm_g, mla_kv_norm_g, mla_w_uq, mla_w_ukv, mla_w_o,
                 s5_a_re, s5_a_im, s5_log_dt, s5_b_re, s5_b_im, s5_c_re, s5_c_im, s5_d, s5_w_glu,
                 dn_conv_w, dn_a_log, dn_dt_bias, dn_out_norm_g, dn_w_o, w_out, cos, sin, cid):
    p = h @ w_in
    y_mla = mla_branch(p[..., OFF_Q:OFF_KV], p[..., OFF_KV:OFF_KR], p[..., OFF_KR:OFF_S5],
                       mla_q_norm_g, mla_kv_norm_g, mla_w_uq, mla_w_ukv, mla_w_o, cos, sin, cid)
    y_s5 = s5_branch(p[..., OFF_S5:OFF_DN_QKV], s5_a_re, s5_a_im, s5_log_dt,
                     s5_b_re, s5_b_im, s5_c_re, s5_c_im, s5_d, s5_w_glu)
    y_dn = gated_deltanet_branch(p[..., OFF_DN_QKV:OFF_DN_Z], p[..., OFF_DN_Z:OFF_DN_A],
                                 p[..., OFF_DN_A:OFF_DN_B], p[..., OFF_DN_B:OFF_GATE],
                                 dn_conv_w, dn_a_log, dn_dt_bias, dn_out_norm_g, dn_w_o)
    g_mla, g_s5, g_dn = jnp.split(jax.nn.sigmoid(p[..., OFF_GATE:]), N_BRANCH, axis=-1)
    merged = g_mla * y_mla + g_s5 * y_s5 + g_dn * y_dn
    return merged @ w_out


def setup_inputs(seed: int = 0) -> dict:
    key = jax.random.key(seed)
    ks = iter(jax.random.split(key, 48))
    f32 = jnp.float32

    def nrm(shape, scale):
        return jax.random.normal(next(ks), shape, f32) * scale

    def gain(shape):
        return 1.0 + 0.02 * jax.random.normal(next(ks), shape, f32)

    def unif(shape, lo, hi):
        return jax.random.uniform(next(ks), shape, f32, lo, hi)

    D, F = D_MODEL, FFN_HIDDEN
    dn_dt = jnp.exp(unif((DEPTH, DN_HEADS), math.log(1e-3), math.log(1e-1)))
    inp = {
        'x': nrm((BATCH, SEQ, D), 1.0),
        'meta_tokens': nrm((N_META, D), 1.0),
        'sandwich_g': gain((DEPTH, 6, D)),
        'ffn1_w13': nrm((DEPTH, D, 2 * F), D ** -0.5),
        'ffn1_w2': nrm((DEPTH, F, D), F ** -0.5),
        'w_in': nrm((DEPTH, D, N_IN), D ** -0.5),
        'mla_q_norm_g': gain((DEPTH, MLA_Q_LORA)),
        'mla_kv_norm_g': gain((DEPTH, MLA_KV_LORA)),
        'mla_w_uq': nrm((DEPTH, MLA_Q_LORA, MLA_HEADS * MLA_QK), MLA_Q_LORA ** -0.5),
        'mla_w_ukv': nrm((DEPTH, MLA_KV_LORA, MLA_HEADS * (MLA_NOPE + MLA_V)), MLA_KV_LORA ** -0.5),
        'mla_w_o': nrm((DEPTH, MLA_WIDTH, D), MLA_WIDTH ** -0.5),
        's5_a_re': -0.5 + nrm((DEPTH, S5_GROUPS, S5_STATE), 0.01),
        's5_a_im': math.pi * jnp.arange(S5_STATE, dtype=f32) + nrm((DEPTH, S5_GROUPS, S5_STATE), 0.01),
        's5_log_dt': unif((DEPTH, S5_GROUPS), math.log(S5_DT_MIN), math.log(S5_DT_MAX)),
        's5_b_re': nrm((DEPTH, S5_GROUPS, S5_STATE, S5_GROUP), (2 * S5_GROUP) ** -0.5),
        's5_b_im': nrm((DEPTH, S5_GROUPS, S5_STATE, S5_GROUP), (2 * S5_GROUP) ** -0.5),
        's5_c_re': nrm((DEPTH, S5_GROUPS, S5_GROUP, S5_STATE), S5_STATE ** -0.5),
        's5_c_im': nrm((DEPTH, S5_GROUPS, S5_GROUP, S5_STATE), S5_STATE ** -0.5),
        's5_d': nrm((DEPTH, S5_GROUPS, S5_GROUP), 0.5),
        's5_w_glu': nrm((DEPTH, S5_WIDTH, 2 * D), S5_WIDTH ** -0.5),
        'dn_conv_w': nrm((DEPTH, DN_CONV, 2 * DN_QK + DN_WIDTH), DN_CONV ** -0.5),
        'dn_a_log': jnp.log(unif((DEPTH, DN_HEADS), 1.0, 16.0)),
        'dn_dt_bias': dn_dt + jnp.log(-jnp.expm1(-dn_dt)),
        'dn_out_norm_g': gain((DEPTH, DN_DV)),
        'dn_w_o': nrm((DEPTH, DN_WIDTH, D), DN_WIDTH ** -0.5),
        'w_out': nrm((DEPTH, D, D), D ** -0.5),
        'ffn2_w13': nrm((DEPTH, D, 2 * F), D ** -0.5),
        'ffn2_w2': nrm((DEPTH, F, D), F ** -0.5),
    }
    return inp


def reference(x, meta_tokens, sandwich_g, ffn1_w13, ffn1_w2, w_in, mla_q_norm_g, mla_kv_norm_g,
              mla_w_uq, mla_w_ukv, mla_w_o, s5_a_re, s5_a_im, s5_log_dt, s5_b_re, s5_b_im,
              s5_c_re, s5_c_im, s5_d, s5_w_glu, dn_conv_w, dn_a_log, dn_dt_bias, dn_out_norm_g,
              dn_w_o, w_out, ffn2_w13, ffn2_w2):
    B = x.shape[0]
    meta = jnp.broadcast_to(meta_tokens[None].astype(x.dtype), (B, N_META, D_MODEL))
    h = jnp.concatenate([meta, x], axis=1)
    L = h.shape[1]
    cos, sin = rope_tables(L)
    cid = chunk_ids(L)
    for l in range(DEPTH):
        g = sandwich_g[l]
        h = h + 0.5 * rms_norm(swiglu(rms_norm(h, g[0]), ffn1_w13[l], ffn1_w2[l]), g[1])
        mix = hybrid_mixer(rms_norm(h, g[2]), w_in[l], mla_q_norm_g[l], mla_kv_norm_g[l],
                           mla_w_uq[l], mla_w_ukv[l], mla_w_o[l], s5_a_re[l], s5_a_im[l],
                           s5_log_dt[l], s5_b_re[l], s5_b_im[l], s5_c_re[l], s5_c_im[l], s5_d[l],
                           s5_w_glu[l], dn_conv_w[l], dn_a_log[l], dn_dt_bias[l],
                           dn_out_norm_g[l], dn_w_o[l], w_out[l], cos, sin, cid)
        h = h + rms_norm(mix, g[3])
        h = h + 0.5 * rms_norm(swiglu(rms_norm(h, g[4]), ffn2_w13[l], ffn2_w2[l]), g[5])
    return h[:, N_META:]
```

```python
import functools
import math

import jax
import jax.numpy as jnp
from jax import lax
from jax.experimental import pallas as pl
from jax.experimental.pallas import tpu as pltpu

F32 = jnp.float32
BF16 = jnp.bfloat16

D_MODEL = 4096
CHUNK = 64
N_META = 16
EPS = 1e-6

MLA_HEADS = 16
MLA_Q_LORA = 1024
MLA_KV_LORA = 512
MLA_NOPE = 128
MLA_ROPE = 64
MLA_V = 128
MLA_QK = MLA_NOPE + MLA_ROPE
MLA_WIDTH = MLA_HEADS * MLA_V
MLA_QPAD = 256
ROPE_THETA = 10000.0

S5_WIDTH = 1024
S5_GROUP = 16
S5_GROUPS = S5_WIDTH // S5_GROUP
S5_STATE = 64
S5_GB = 4
S5_GPB = S5_GROUPS // S5_GB

DN_HEADS = 8
DN_DK = 128
DN_DV = 128
DN_QK = DN_HEADS * DN_DK
DN_WIDTH = DN_HEADS * DN_DV
DN_CONV = 4

FFN_HIDDEN = 11008
FFN_PAD = 11264

OFF_Q = 0
OFF_KV = OFF_Q + MLA_Q_LORA
OFF_KR = OFF_KV + MLA_KV_LORA
OFF_S5 = OFF_KR + MLA_ROPE
OFF_DN_QKV = OFF_S5 + S5_WIDTH
OFF_DN_Z = OFF_DN_QKV + 2 * DN_QK + DN_WIDTH
OFF_DN_A = OFF_DN_Z + DN_WIDTH
OFF_DN_B = OFF_DN_A + DN_HEADS
OFF_GATE = OFF_DN_B + DN_HEADS

P_Q = 0
P_KV = P_Q + MLA_Q_LORA
P_S5 = P_KV + MLA_KV_LORA
P_DNQKV = P_S5 + S5_WIDTH
P_DNZ = P_DNQKV + 3 * DN_QK
P_KR = P_DNZ + DN_WIDTH
P_AB = P_KR + 128
P_WIDTH = P_AB + 128

VMEM_LIMIT_BYTES = 56 * 1024 * 1024


def _cparams(*sem):
    return pltpu.CompilerParams(dimension_semantics=sem, vmem_limit_bytes=VMEM_LIMIT_BYTES)


def _pick(n, candidates):
    for c in candidates:
        if n % c == 0:
            return c
    raise ValueError(f"no tile in {candidates} divides {n}")


def _rms(x, g):
    return x * lax.rsqrt(jnp.mean(x * x, axis=-1, keepdims=True) + EPS) * g


def _rmsnorm_kernel(x_ref, g_ref, o_ref):
    o_ref[...] = _rms(x_ref[...], g_ref[...]).astype(o_ref.dtype)


def rmsnorm_cast(x, g, *, col_block, width, tm):
    M = x.shape[0]
    return pl.pallas_call(
        _rmsnorm_kernel,
        grid=(M // tm,),
        in_specs=[pl.BlockSpec((tm, width), lambda i: (i, col_block)),
                  pl.BlockSpec((1, width), lambda i: (0, 0))],
        out_specs=pl.BlockSpec((tm, width), lambda i: (i, 0)),
        out_shape=jax.ShapeDtypeStruct((M, width), BF16),
        compiler_params=_cparams("parallel"),
        name="rmsnorm_cast",
    )(x, g.reshape(1, width).astype(F32))


def _resid_norm_kernel(h_ref, y_ref, gpost_ref, gpre_ref, ho_ref, xn_ref, *, coef):
    h = h_ref[...] + coef * _rms(y_ref[...], gpost_ref[...])
    ho_ref[...] = h
    xn_ref[...] = _rms(h, gpre_ref[...]).astype(xn_ref.dtype)


def _resid_kernel(h_ref, y_ref, gpost_ref, ho_ref, *, coef):
    ho_ref[...] = h_ref[...] + coef * _rms(y_ref[...], gpost_ref[...])


def resid_norm(h, y, g_post, g_pre, *, coef, tm):
    M, D = h.shape
    row = pl.BlockSpec((tm, D), lambda i: (i, 0))
    gain = pl.BlockSpec((1, D), lambda i: (0, 0))
    if g_pre is None:
        return pl.pallas_call(
            functools.partial(_resid_kernel, coef=coef),
            grid=(M // tm,), in_specs=[row, row, gain], out_specs=row,
            out_shape=jax.ShapeDtypeStruct((M, D), F32),
            input_output_aliases={0: 0},
            compiler_params=_cparams("parallel"), name="resid",
        )(h, y, g_post.reshape(1, D)), None
    return pl.pallas_call(
        functools.partial(_resid_norm_kernel, coef=coef),
        grid=(M // tm,), in_specs=[row, row, gain, gain], out_specs=[row, row],
        out_shape=[jax.ShapeDtypeStruct((M, D), F32), jax.ShapeDtypeStruct((M, D), BF16)],
        input_output_aliases={0: 0},
        compiler_params=_cparams("parallel"), name="resid_norm",
    )(h, y, g_post.reshape(1, D), g_pre.reshape(1, D))


def _mm_kernel(x_ref, w_ref, o_ref, *scratch, nk, act):
    part = jnp.dot(x_ref[...], w_ref[...], preferred_element_type=F32)

    def finish(acc):
        if act == "sigmoid":
            acc = jax.nn.sigmoid(acc)
        o_ref[...] = acc.astype(o_ref.dtype)

    if nk == 1:
        finish(part)
        return
    acc_ref = scratch[0]
    k = pl.program_id(2)

    @pl.when(k == 0)
    def _():
        acc_ref[...] = part

    @pl.when(k > 0)
    def _():
        acc_ref[...] += part

    @pl.when(k == nk - 1)
    def _():
        finish(acc_ref[...])


def matmul(x, w, *, tm, tn, tk, out_dtype, act=None, name="matmul"):
    M, K = x.shape
    N = w.shape[1]
    nk = K // tk
    scratch = [pltpu.VMEM((tm, tn), F32)] if nk > 1 else []
    return pl.pallas_call(
        functools.partial(_mm_kernel, nk=nk, act=act),
        grid=(M // tm, N // tn, nk),
        in_specs=[pl.BlockSpec((tm, tk), lambda i, j, k: (i, k)),
                  pl.BlockSpec((tk, tn), lambda i, j, k: (k, j))],
        out_specs=pl.BlockSpec((tm, tn), lambda i, j, k: (i, j)),
        out_shape=jax.ShapeDtypeStruct((M, N), out_dtype),
        scratch_shapes=scratch,
        compiler_params=_cparams("parallel", "parallel", "arbitrary"),
        name=name,
    )(x, w)


def _mm_swiglu_kernel(x_ref, wa_ref, wb_ref, o_ref):
    x = x_ref[...]
    a = jnp.dot(x, wa_ref[...], preferred_element_type=F32)
    b = jnp.dot(x, wb_ref[...], preferred_element_type=F32)
    o_ref[...] = (jax.nn.silu(a) * b).astype(o_ref.dtype)


def matmul_swiglu(x, wa, wb, *, tm, tn):
    M, K = x.shape
    N = wa.shape[1]
    wspec = pl.BlockSpec((K, tn), lambda i, j: (0, j))
    return pl.pallas_call(
        _mm_swiglu_kernel,
        grid=(M // tm, N // tn),
        in_specs=[pl.BlockSpec((tm, K), lambda i, j: (i, 0)), wspec, wspec],
        out_specs=pl.BlockSpec((tm, tn), lambda i, j: (i, j)),
        out_shape=jax.ShapeDtypeStruct((M, N), BF16),
        compiler_params=_cparams("parallel", "parallel"),
        name="ffn_up_swiglu",
    )(x, wa, wb)


def _rope_block(a, c, s):
    return a * c + pltpu.roll(a, 64, axis=1) * s


def _mm_qrope_kernel(x_ref, w_ref, c_ref, s_ref, o_ref, *, heads, scale):
    acc = jnp.dot(x_ref[...], w_ref[...], preferred_element_type=F32) * scale
    c = c_ref[...]
    s = s_ref[...]
    for hb in range(heads):
        lo = hb * MLA_QPAD
        o_ref[:, lo:lo + 128] = acc[:, lo:lo + 128].astype(o_ref.dtype)
        o_ref[:, lo + 128:lo + 256] = _rope_block(acc[:, lo + 128:lo + 256], c, s).astype(o_ref.dtype)


def matmul_qrope(x, w, cos_t, sin_t, *, tm, heads_per_tile, scale):
    M, K = x.shape
    N = w.shape[1]
    tn = heads_per_tile * MLA_QPAD
    tab = pl.BlockSpec((tm, 128), lambda i, j: (i, 0))
    return pl.pallas_call(
        functools.partial(_mm_qrope_kernel, heads=heads_per_tile, scale=scale),
        grid=(M // tm, N // tn),
        in_specs=[pl.BlockSpec((tm, K), lambda i, j: (i, 0)),
                  pl.BlockSpec((K, tn), lambda i, j: (0, j)), tab, tab],
        out_specs=pl.BlockSpec((tm, tn), lambda i, j: (i, j)),
        out_shape=jax.ShapeDtypeStruct((M, N), BF16),
        compiler_params=_cparams("parallel", "parallel"),
        name="mla_q_proj_rope",
    )(x, w, cos_t, sin_t)


def _krope_kernel(x_ref, c_ref, s_ref, o_ref):
    o_ref[...] = _rope_block(x_ref[...], c_ref[...], s_ref[...]).astype(o_ref.dtype)


def k_rope(p, cos_t, sin_t, *, tm):
    M = p.shape[0]
    tab = pl.BlockSpec((tm, 128), lambda i: (i, 0))
    return pl.pallas_call(
        _krope_kernel,
        grid=(M // tm,),
        in_specs=[pl.BlockSpec((tm, 128), lambda i: (i, P_KR // 128)), tab, tab],
        out_specs=tab,
        out_shape=jax.ShapeDtypeStruct((M, 128), BF16),
        compiler_params=_cparams("parallel"),
        name="mla_k_rope",
    )(p, cos_t, sin_t)


def _merge_kernel(o_ref, hs_ref, dn_ref, wo_ref, wv_ref, wg_ref, wd_ref,
                  g0_ref, g1_ref, g2_ref, out_ref):
    hs = hs_ref[...]
    y_mla = jnp.dot(o_ref[...], wo_ref[...], preferred_element_type=F32)
    val = jnp.dot(hs, wv_ref[...], preferred_element_type=F32)
    gate = jnp.dot(hs, wg_ref[...], preferred_element_type=F32)
    y_dn = jnp.dot(dn_ref[...], wd_ref[...], preferred_element_type=F32)
    merged = (g0_ref[...].astype(F32) * y_mla
              + g1_ref[...].astype(F32) * (val * jax.nn.sigmoid(gate))
              + g2_ref[...].astype(F32) * y_dn)
    out_ref[...] = merged.astype(out_ref.dtype)


def merge_branches(o_mla, h_s5, o_dn, w_o, w_glu_val, w_glu_gate, w_dn_o, gates, *, tm, tn):
    M = o_mla.shape[0]
    D = w_o.shape[1]
    nj = D // tn

    def rows(width):
        return pl.BlockSpec((tm, width), lambda i, j: (i, 0))

    def cols(kdim):
        return pl.BlockSpec((kdim, tn), lambda i, j: (0, j))

    def gate(b):
        return pl.BlockSpec((tm, tn), lambda i, j: (i, b * nj + j))

    return pl.pallas_call(
        _merge_kernel,
        grid=(M // tm, nj),
        in_specs=[rows(MLA_WIDTH), rows(S5_WIDTH), rows(DN_WIDTH),
                  cols(MLA_WIDTH), cols(S5_WIDTH), cols(S5_WIDTH), cols(DN_WIDTH),
                  gate(0), gate(1), gate(2)],
        out_specs=pl.BlockSpec((tm, tn), lambda i, j: (i, j)),
        out_shape=jax.ShapeDtypeStruct((M, D), BF16),
        compiler_params=_cparams("parallel", "parallel"),
        name="merge_branches",
    )(o_mla, h_s5, o_dn, w_o, w_glu_val, w_glu_gate, w_dn_o, gates, gates, gates)


_NEG = -1e30


def _chunk_id(pos):
    return (pos + (CHUNK - N_META)) >> 6


def _attn_kernel(q_ref, kn_ref, kr_ref, v_ref, o_ref, *, tq, tk, seq_pad):
    q0 = pl.program_id(2) * tq
    q = q_ref[...]
    cq = _chunk_id(q0 + lax.broadcasted_iota(jnp.int32, (tq, 1), 0))
    full_end = (_chunk_id(q0) + 1) * CHUNK - (CHUNK - N_META)
    last_end = jnp.minimum((_chunk_id(q0 + tq - 1) + 1) * CHUNK - (CHUNK - N_META), seq_pad)
    n_full = full_end // tk
    n_all = (last_end + tk - 1) // tk

    def step(j, carry, masked):
        m, l, acc = carry
        ks = pl.multiple_of(j * tk, tk)
        k = jnp.concatenate([kn_ref[pl.ds(ks, tk), :], kr_ref[pl.ds(ks, tk), :]], axis=1)
        s = lax.dot_general(q, k, (((1,), (1,)), ((), ())), preferred_element_type=F32)
        if masked:
            ck = _chunk_id(ks + lax.broadcasted_iota(jnp.int32, (1, tk), 1))
            s = jnp.where(cq >= ck, s, _NEG)
        m_new = jnp.maximum(m, jnp.max(s, axis=-1, keepdims=True))
        alpha = jnp.exp(m - m_new)
        p = jnp.exp(s - m_new)
        l = alpha * l + jnp.sum(p, axis=-1, keepdims=True)
        acc = alpha * acc + jnp.dot(p.astype(BF16), v_ref[pl.ds(ks, tk), :],
                                    preferred_element_type=F32)
        return m_new, l, acc

    init = (jnp.full((tq, 1), _NEG, F32), jnp.zeros((tq, 1), F32), jnp.zeros((tq, MLA_V), F32))
    carry = lax.fori_loop(0, n_full, functools.partial(step, masked=False), init)
    _, l, acc = lax.fori_loop(n_full, n_all, functools.partial(step, masked=True), carry)
    o_ref[...] = (acc / l).astype(o_ref.dtype)


def mla_attention(q, kv, kr, *, batch, seq_pad, tq, tk):
    M = q.shape[0]
    nq = seq_pad // tq
    H = MLA_HEADS
    return pl.pallas_call(
        functools.partial(_attn_kernel, tq=tq, tk=tk, seq_pad=seq_pad),
        grid=(batch, H, nq),
        in_specs=[pl.BlockSpec((tq, MLA_QPAD), lambda b, h, i: (b * nq + i, h)),
                  pl.BlockSpec((seq_pad, 128), lambda b, h, i: (b, h)),
                  pl.BlockSpec((seq_pad, 128), lambda b, h, i: (b, 0)),
                  pl.BlockSpec((seq_pad, 128), lambda b, h, i: (b, H + h))],
        out_specs=pl.BlockSpec((tq, MLA_V), lambda b, h, i: (b * nq + i, h)),
        out_shape=jax.ShapeDtypeStruct((M, MLA_WIDTH), BF16),
        compiler_params=_cparams("parallel", "parallel", "parallel"),
        name="mla_attention",
    )(q, kv, kr, kv)


def _s5_kernel(u_ref, br_ref, bi_ref, cr_ref, ci_ref, d_ref, ast_ref, pw_ref, o_ref,
               xr_s, xi_s, car_s, *, tt):
    n = S5_GPB * S5_STATE

    @pl.when(pl.program_id(2) == 0)
    def _():
        car_s[...] = jnp.zeros_like(car_s)

    u = u_ref[...]
    ub = u.astype(BF16)
    xr = jnp.dot(ub, br_ref[...], preferred_element_type=F32).reshape(tt // 8, 8, n)
    xi = jnp.dot(ub, bi_ref[...], preferred_element_type=F32).reshape(tt // 8, 8, n)
    for si in range(3):
        ar = ast_ref[2 * si]
        ai = ast_ref[2 * si + 1]
        sr = pltpu.roll(xr, 1 << si, axis=1)
        sm = pltpu.roll(xi, 1 << si, axis=1)
        xr, xi = xr + (ar * sr - ai * sm), xi + (ar * sm + ai * sr)
    xr_s[...] = xr.reshape(tt, n)
    xi_s[...] = xi.reshape(tt, n)
    pr = pw_ref[0]
    pi = pw_ref[1]

    def body(i, carry):
        cr, ci = carry
        sl = pl.ds(pl.multiple_of(i * 8, 8), 8)
        a = xr_s[sl, :] + (pr * cr - pi * ci)
        b = xi_s[sl, :] + (pr * ci + pi * cr)
        xr_s[sl, :] = a
        xi_s[sl, :] = b
        return (jnp.broadcast_to(a[7:8, :], (8, n)), jnp.broadcast_to(b[7:8, :], (8, n)))

    cr, ci = lax.fori_loop(0, tt // 8, body, (car_s[0], car_s[1]))
    car_s[0] = cr
    car_s[1] = ci
    y = (jnp.dot(xr_s[...].astype(BF16), cr_ref[...], preferred_element_type=F32)
         - jnp.dot(xi_s[...].astype(BF16), ci_ref[...], preferred_element_type=F32)
         + d_ref[...] * u)
    o_ref[...] = jax.nn.gelu(y).astype(o_ref.dtype)


def s5_scan(p, s5c, *, batch, seq_pad, tt):
    M = p.shape[0]
    nt = seq_pad // tt
    cw = S5_GPB * S5_GROUP
    n = S5_GPB * S5_STATE
    ublk = P_S5 // cw
    return pl.pallas_call(
        functools.partial(_s5_kernel, tt=tt),
        grid=(batch, S5_GB, nt),
        in_specs=[pl.BlockSpec((tt, cw), lambda b, g, t: (b * nt + t, ublk + g)),
                  pl.BlockSpec((None, cw, n), lambda b, g, t: (g, 0, 0)),
                  pl.BlockSpec((None, cw, n), lambda b, g, t: (g, 0, 0)),
                  pl.BlockSpec((None, n, cw), lambda b, g, t: (g, 0, 0)),
                  pl.BlockSpec((None, n, cw), lambda b, g, t: (g, 0, 0)),
                  pl.BlockSpec((None, 1, cw), lambda b, g, t: (g, 0, 0)),
                  pl.BlockSpec((None, 6, 8, n), lambda b, g, t: (g, 0, 0, 0)),
                  pl.BlockSpec((None, 2, 8, n), lambda b, g, t: (g, 0, 0, 0))],
        out_specs=pl.BlockSpec((tt, cw), lambda b, g, t: (b * nt + t, g)),
        out_shape=jax.ShapeDtypeStruct((M, S5_WIDTH), BF16),
        scratch_shapes=[pltpu.VMEM((tt, n), F32), pltpu.VMEM((tt, n), F32),
                        pltpu.VMEM((2, 8, n), F32)],
        compiler_params=_cparams("parallel", "parallel", "arbitrary"),
        name="s5_scan",
    )(p, s5c["br"], s5c["bi"], s5c["cr"], s5c["ci"], s5c["d"], s5c["ast"], s5c["pw"])


def s5_constants(a_re, a_im, log_dt, b_re, b_im, c_re, c_im, d):
    ar, ai = a_re.astype(F32), a_im.astype(F32)
    delta = jnp.exp(log_dt.astype(F32))[:, None]
    mag = jnp.exp(ar * delta)
    abar_r, abar_i = mag * jnp.cos(ai * delta), mag * jnp.sin(ai * delta)
    den = ar * ar + ai * ai
    zr = ((abar_r - 1.0) * ar + abar_i * ai) / den
    zi = (abar_i * ar - (abar_r - 1.0) * ai) / den
    br, bi = b_re.astype(F32), b_im.astype(F32)
    bbar_r = zr[..., None] * br - zi[..., None] * bi
    bbar_i = zr[..., None] * bi + zi[..., None] * br

    eye = jnp.eye(S5_GPB, dtype=F32)

    def in_blockdiag(m):
        m = m.reshape(S5_GB, S5_GPB, S5_STATE, S5_GROUP)
        bd = jnp.einsum("bgpc,gh->bgchp", m, eye)
        return bd.reshape(S5_GB, S5_GPB * S5_GROUP, S5_GPB * S5_STATE).astype(BF16)

    def out_blockdiag(m):
        m = m.reshape(S5_GB, S5_GPB, S5_GROUP, S5_STATE)
        bd = jnp.einsum("bgcp,gh->bgphc", m, eye)
        return bd.reshape(S5_GB, S5_GPB * S5_STATE, S5_GPB * S5_GROUP).astype(BF16)

    def cmul(x, y):
        return x[0] * y[0] - x[1] * y[1], x[0] * y[1] + x[1] * y[0]

    n = S5_GPB * S5_STATE
    a1 = (abar_r.reshape(S5_GB, n), abar_i.reshape(S5_GB, n))
    a2 = cmul(a1, a1)
    a4 = cmul(a2, a2)
    row = jnp.arange(8)[None, :, None]
    ast = []
    for s, a in ((1, a1), (2, a2), (4, a4)):
        for part in a:
            ast.append(jnp.where(row >= s, part[:, None, :], 0.0))
    ast = jnp.stack(ast, axis=1)
    pows = [a1]
    for _ in range(7):
        pows.append(cmul(pows[-1], a1))
    pw = jnp.stack([jnp.stack([p[0] for p in pows], axis=1),
                    jnp.stack([p[1] for p in pows], axis=1)], axis=1)
    return {
        "br": in_blockdiag(bbar_r), "bi": in_blockdiag(bbar_i),
        "cr": out_blockdiag(c_re.astype(F32)), "ci": out_blockdiag(c_im.astype(F32)),
        "d": d.astype(F32).reshape(S5_GB, 1, S5_GPB * S5_GROUP),
        "ast": ast, "pw": pw,
    }


def _dnconv_kernel(x_ref, w_ref, o_ref, buf, *, tt):
    @pl.when(pl.program_id(2) == 0)
    def _():
        buf[0:8, :] = jnp.zeros((8, buf.shape[1]), F32)

    buf[8:8 + tt, :] = x_ref[...]
    acc = buf[8:8 + tt, :] * w_ref[3:4, :]
    for j in range(DN_CONV - 1):
        lo = 8 - (DN_CONV - 1) + j
        acc = acc + buf[lo:lo + tt, :] * w_ref[j:j + 1, :]
    o_ref[...] = jax.nn.silu(acc)
    buf[0:8, :] = buf[tt:tt + 8, :]


def dn_conv(p, conv_w, *, batch, seq_pad, tt, tc):
    M = p.shape[0]
    nt = seq_pad // tt
    width = conv_w.shape[1]
    cblk = P_DNQKV // tc
    return pl.pallas_call(
        functools.partial(_dnconv_kernel, tt=tt),
        grid=(batch, width // tc, nt),
        in_specs=[pl.BlockSpec((tt, tc), lambda b, c, t: (b * nt + t, cblk + c)),
                  pl.BlockSpec((DN_CONV, tc), lambda b, c, t: (0, c))],
        out_specs=pl.BlockSpec((tt, tc), lambda b, c, t: (b * nt + t, c)),
        out_shape=jax.ShapeDtypeStruct((M, width), F32),
        scratch_shapes=[pltpu.VMEM((tt + 8, tc), F32)],
        compiler_params=_cparams("parallel", "parallel", "arbitrary"),
        name="dn_conv_silu",
    )(p, conv_w.astype(F32))


def _dn_gates_kernel(ab_ref, alog_ref, dtb_ref, gcum_ref, beta_ref):
    ab = ab_ref[...]
    g = -jnp.exp(alog_ref[...]) * jax.nn.softplus(ab[:DN_HEADS] + dtb_ref[...])
    beta_ref[...] = jax.nn.sigmoid(ab[DN_HEADS:])
    lane = lax.broadcasted_iota(jnp.int32, g.shape, 1) % CHUNK
    s = 1
    while s < CHUNK:
        g = g + jnp.where(lane >= s, pltpu.roll(g, s, axis=1), 0.0)
        s *= 2
    gcum_ref[...] = g


def dn_gates(ab_t, a_log, dt_bias):
    B, _, Lp = ab_t.shape
    col = pl.BlockSpec((DN_HEADS, 1), lambda b: (0, 0))
    out = pl.BlockSpec((None, DN_HEADS, Lp), lambda b: (b, 0, 0))
    return pl.pallas_call(
        _dn_gates_kernel,
        grid=(B,),
        in_specs=[pl.BlockSpec((None, 2 * DN_HEADS, Lp), lambda b: (b, 0, 0)), col, col],
        out_specs=[out, out],
        out_shape=[jax.ShapeDtypeStruct((B, DN_HEADS, Lp), F32)] * 2,
        compiler_params=_cparams("parallel"),
        name="dn_gates",
    )(ab_t, a_log.astype(F32).reshape(DN_HEADS, 1), dt_bias.astype(F32).reshape(DN_HEADS, 1))


def _dot(a, b):
    return jnp.dot(a, b, preferred_element_type=F32)


def _dot_nt(a, b):
    return lax.dot_general(a, b, (((1,), (1,)), ((), ())), preferred_element_type=F32)


def _dot_tn(a, b):
    return lax.dot_general(a, b, (((0,), (0,)), ((), ())), preferred_element_type=F32)


def _l2(x):
    return x * lax.rsqrt(jnp.sum(x * x, axis=-1, keepdims=True) + EPS)


def _dn_kernel(q_ref, k_ref, v_ref, z_ref, bcol_ref, gcol_ref, grow_ref, gn_ref, o_ref, *, n_chunks):
    C = CHUNK
    ii = lax.broadcasted_iota(jnp.int32, (C, C), 0)
    jj = lax.broadcasted_iota(jnp.int32, (C, C), 1)
    tri = ii >= jj
    strict = ii > jj
    eye = (ii == jj).astype(F32)
    gn = gn_ref[...]

    def body(c, S):
        sl = pl.ds(pl.multiple_of(c * C, C), C)
        q = _l2(q_ref[sl, :]) * (DN_DK ** -0.5)
        k = _l2(k_ref[sl, :])
        v = v_ref[sl, :]
        beta = bcol_ref[sl, :]
        gc = gcol_ref[sl, :]
        gr = grow_ref[pl.ds(c, 1), :]
        decay = jnp.exp(jnp.where(tri, gc - gr, -jnp.inf))
        kb = k * beta
        vb = v * beta
        a_mat = jnp.where(strict, _dot_nt(kb, k) * decay, 0.0)
        t_inv = eye - a_mat
        pw = a_mat
        for _ in range(5):
            pw = _dot(pw, pw)
            t_inv = t_inv + _dot(t_inv, pw)
        u = _dot(t_inv, vb)
        w = _dot(t_inv, kb * jnp.exp(gc))
        qk = _dot_nt(q, k) * decay
        v_new = u - _dot(w, S)
        o = _dot(q * jnp.exp(gc), S) + _dot(qk, v_new)
        g_last = gc[C - 1:C, :]
        S = S * jnp.exp(g_last) + _dot_tn(k * jnp.exp(g_last - gc), v_new)
        o_ref[sl, :] = (_rms(o, gn) * jax.nn.silu(z_ref[sl, :])).astype(o_ref.dtype)
        return S

    lax.fori_loop(0, n_chunks, body, jnp.zeros((DN_DK, DN_DV), F32))


def deltanet(qkv, p, beta_col, gcum_col, gcum_row, out_norm_g, *, batch, seq_pad):
    M = p.shape[0]
    H = DN_HEADS
    nc = seq_pad // CHUNK

    def head(off):
        return pl.BlockSpec((seq_pad, 128), lambda b, h: (b, off + h))

    colspec = pl.BlockSpec((None, None, seq_pad, 1), lambda b, h: (b, h, 0, 0))
    return pl.pallas_call(
        functools.partial(_dn_kernel, n_chunks=nc),
        grid=(batch, H),
        in_specs=[head(0), head(H), head(2 * H),
                  pl.BlockSpec((seq_pad, 128), lambda b, h: (b, P_DNZ // 128 + h)),
                  colspec, colspec,
                  pl.BlockSpec((None, None, nc, CHUNK), lambda b, h: (b, h, 0, 0)),
                  pl.BlockSpec((1, DN_DV), lambda b, h: (0, 0))],
        out_specs=pl.BlockSpec((seq_pad, 128), lambda b, h: (b, h)),
        out_shape=jax.ShapeDtypeStruct((M, DN_WIDTH), BF16),
        compiler_params=_cparams("parallel", "parallel"),
        name="deltanet",
    )(qkv, qkv, qkv, p, beta_col, gcum_col, gcum_row, out_norm_g.astype(F32).reshape(1, DN_DV))


def _rot_cols(w):
    half = w.shape[1] // 2
    return jnp.concatenate([-w[:, half:], w[:, :half]], axis=1)


def _prep_layer(w_in, mla_w_uq, mla_w_ukv, ffn_w13, ffn_w2):
    kr = w_in[:, OFF_KR:OFF_S5]
    w_small = jnp.concatenate([
        w_in[:, OFF_Q:OFF_KR], w_in[:, OFF_S5:OFF_DN_A], kr, _rot_cols(kr),
        w_in[:, OFF_DN_A:OFF_GATE], jnp.zeros((D_MODEL, 128 - 2 * DN_HEADS), w_in.dtype)],
        axis=1).astype(BF16)
    w_gate = w_in[:, OFF_GATE:].astype(BF16)
    uq = mla_w_uq.reshape(MLA_Q_LORA, MLA_HEADS, MLA_QK)
    uq_rope = uq[..., MLA_NOPE:].reshape(MLA_Q_LORA * MLA_HEADS, MLA_ROPE)
    uq_rot = _rot_cols(uq_rope).reshape(MLA_Q_LORA, MLA_HEADS, MLA_ROPE)
    w_uq = jnp.concatenate([uq, uq_rot], axis=-1).reshape(MLA_Q_LORA, MLA_HEADS * MLA_QPAD).astype(BF16)
    ukv = mla_w_ukv.reshape(MLA_KV_LORA, MLA_HEADS, 2, 128)
    w_ukv = jnp.transpose(ukv, (0, 2, 1, 3)).reshape(MLA_KV_LORA, 2 * MLA_HEADS * 128).astype(BF16)
    return w_small, w_gate, w_uq, w_ukv


def _prep_ffn(w13, w2):
    pad = FFN_PAD - FFN_HIDDEN
    wa = jnp.pad(w13[:, :FFN_HIDDEN], ((0, 0), (0, pad))).astype(BF16)
    wb = jnp.pad(w13[:, FFN_HIDDEN:], ((0, 0), (0, pad))).astype(BF16)
    wd = jnp.pad(w2, ((0, pad), (0, 0))).astype(BF16)
    return wa, wb, wd


def _rope_tables(seq_pad, batch):
    inv = ROPE_THETA ** (-jnp.arange(0, MLA_ROPE, 2, dtype=F32) / MLA_ROPE)
    ang = jnp.arange(seq_pad, dtype=F32)[:, None] * inv[None, :]
    zeros = jnp.zeros((seq_pad, 128 - MLA_ROPE), F32)
    cos_t = jnp.concatenate([jnp.cos(ang), jnp.cos(ang), zeros], axis=1)
    sin_t = jnp.concatenate([jnp.sin(ang), jnp.sin(ang), zeros], axis=1)
    return jnp.tile(cos_t, (batch, 1)), jnp.tile(sin_t, (batch, 1))


def _ffn(xn, wa, wb, wd, *, tm):
    hid = matmul_swiglu(xn, wa, wb, tm=tm, tn=512)
    return matmul(hid, wd, tm=tm, tn=1024, tk=FFN_PAD // 4, out_dtype=F32, name="ffn_down")


def kernel(x, meta_tokens, sandwich_g, ffn1_w13, ffn1_w2, w_in, mla_q_norm_g, mla_kv_norm_g, mla_w_uq, mla_w_ukv, mla_w_o, s5_a_re, s5_a_im, s5_log_dt, s5_b_re, s5_b_im, s5_c_re, s5_c_im, s5_d, s5_w_glu, dn_conv_w, dn_a_log, dn_dt_bias, dn_out_norm_g, dn_w_o, w_out, ffn2_w13, ffn2_w2):
    B, seq, D = x.shape
    depth = w_in.shape[0]
    L = N_META + seq
    assert (L - N_META) % CHUNK == 0, "pad keys are hidden by the chunk mask only on a chunk boundary"
    Lp = -(-L // 128) * 128
    M = B * Lp
    tm = _pick(M, (1056, 768, 384, 128))
    tr = _pick(M, (176, 128))
    tt = _pick(Lp, (528, 384, 128))
    ta = _pick(Lp, (384, 128))

    meta = jnp.broadcast_to(meta_tokens[None].astype(x.dtype), (B, N_META, D))
    h = jnp.concatenate([meta, x, jnp.zeros((B, Lp - L, D), x.dtype)], axis=1).reshape(M, D)
    cos_t, sin_t = _rope_tables(Lp, B)

    xn = rmsnorm_cast(h, sandwich_g[0, 0], col_block=0, width=D, tm=tr)
    for l in range(depth):
        g = sandwich_g[l]
        y = _ffn(xn, *_prep_ffn(ffn1_w13[l], ffn1_w2[l]), tm=tm)
        h, xn = resid_norm(h, y, g[1], g[2], coef=0.5, tm=tr)

        w_small, w_gate, w_uq, w_ukv = _prep_layer(w_in[l], mla_w_uq[l], mla_w_ukv[l], None, None)
        p = matmul(xn, w_small, tm=tm, tn=768, tk=D, out_dtype=F32, name="in_proj_small")
        gates = matmul(xn, w_gate, tm=tm, tn=1024, tk=D, out_dtype=BF16, act="sigmoid",
                       name="in_proj_gates")

        qn = rmsnorm_cast(p, mla_q_norm_g[l], col_block=P_Q // MLA_Q_LORA, width=MLA_Q_LORA, tm=tm)
        kvn = rmsnorm_cast(p, mla_kv_norm_g[l], col_block=P_KV // MLA_KV_LORA, width=MLA_KV_LORA, tm=tm)
        q = matmul_qrope(qn, w_uq, cos_t, sin_t, tm=tm, heads_per_tile=4, scale=MLA_QK ** -0.5)
        kv = matmul(kvn, w_ukv, tm=tm, tn=1024, tk=MLA_KV_LORA, out_dtype=BF16, name="mla_kv_proj")
        kr = k_rope(p, cos_t, sin_t, tm=tm)
        o_mla = mla_attention(q, kv, kr, batch=B, seq_pad=Lp, tq=ta, tk=ta)

        s5c = s5_constants(s5_a_re[l], s5_a_im[l], s5_log_dt[l], s5_b_re[l], s5_b_im[l],
                           s5_c_re[l], s5_c_im[l], s5_d[l])
        h_s5 = s5_scan(p, s5c, batch=B, seq_pad=Lp, tt=tt)

        qkv = dn_conv(p, dn_conv_w[l], batch=B, seq_pad=Lp, tt=tt, tc=512)
        ab_t = jnp.transpose(p[:, P_AB:P_AB + 2 * DN_HEADS].reshape(B, Lp, 2 * DN_HEADS), (0, 2, 1))
        gcum, beta = dn_gates(ab_t, dn_a_log[l], dn_dt_bias[l])
        o_dn = deltanet(qkv, p, beta[..., None], gcum[..., None],
                        gcum.reshape(B, DN_HEADS, Lp // CHUNK, CHUNK), dn_out_norm_g[l],
                        batch=B, seq_pad=Lp)

        merged = merge_branches(o_mla, h_s5, o_dn, mla_w_o[l].astype(BF16),
                                s5_w_glu[l, :, :D].astype(BF16), s5_w_glu[l, :, D:].astype(BF16),
                                dn_w_o[l].astype(BF16), gates, tm=tm, tn=512)
        mix = matmul(merged, w_out[l].astype(BF16), tm=tm, tn=1024, tk=D, out_dtype=F32, name="out_proj")
        h, xn = resid_norm(h, mix, g[3], g[4], coef=1.0, tm=tr)

        y = _ffn(xn, *_prep_ffn(ffn2_w13[l], ffn2_w2[l]), tm=tm)
        g_next = sandwich_g[l + 1, 0] if l + 1 < depth else None
        h, xn = resid_norm(h, y, g[5], g_next, coef=0.5, tm=tr)
    return h.reshape(B, Lp, D)[:, N_META:L]
```

```python
import functools
import math

import jax
import jax.numpy as jnp
from jax import lax
from jax.experimental import pallas as pl
from jax.experimental.pallas import tpu as pltpu

F32 = jnp.float32
BF16 = jnp.bfloat16

D_MODEL = 4096
CHUNK = 64
N_META = 16
EPS = 1e-6

MLA_HEADS = 16
MLA_Q_LORA = 1024
MLA_KV_LORA = 512
MLA_NOPE = 128
MLA_ROPE = 64
MLA_V = 128
MLA_QK = MLA_NOPE + MLA_ROPE
MLA_WIDTH = MLA_HEADS * MLA_V
MLA_QPAD = 256
ROPE_THETA = 10000.0

S5_WIDTH = 1024
S5_GROUP = 16
S5_GROUPS = S5_WIDTH // S5_GROUP
S5_STATE = 64
S5_GB = 4
S5_GPB = S5_GROUPS // S5_GB

DN_HEADS = 8
DN_DK = 128
DN_DV = 128
DN_QK = DN_HEADS * DN_DK
DN_WIDTH = DN_HEADS * DN_DV
DN_CONV = 4

FFN_HIDDEN = 11008
FFN_PAD = 11264

OFF_Q = 0
OFF_KV = OFF_Q + MLA_Q_LORA
OFF_KR = OFF_KV + MLA_KV_LORA
OFF_S5 = OFF_KR + MLA_ROPE
OFF_DN_QKV = OFF_S5 + S5_WIDTH
OFF_DN_Z = OFF_DN_QKV + 2 * DN_QK + DN_WIDTH
OFF_DN_A = OFF_DN_Z + DN_WIDTH
OFF_DN_B = OFF_DN_A + DN_HEADS
OFF_GATE = OFF_DN_B + DN_HEADS

P_Q = 0
P_DNZ = P_Q + MLA_Q_LORA
P_S5 = P_DNZ + DN_WIDTH
P_DNQKV = P_S5 + S5_WIDTH
P_KV = P_DNQKV + 3 * DN_QK
P_KR = P_KV + MLA_KV_LORA
P_AB = P_KR + 128
P_WIDTH = P_AB + 128

VMEM_LIMIT_BYTES = 56 * 1024 * 1024


def _cparams(*sem):
    return pltpu.CompilerParams(dimension_semantics=sem, vmem_limit_bytes=VMEM_LIMIT_BYTES)


def _pick(n, candidates):
    for c in candidates:
        if n % c == 0:
            return c
    raise ValueError(f"no tile in {candidates} divides {n}")


def _rms(x, g):
    return x * lax.rsqrt(jnp.mean(x * x, axis=-1, keepdims=True) + EPS) * g


def _rmsnorm_kernel(x_ref, g_ref, o_ref):
    o_ref[...] = _rms(x_ref[...], g_ref[...]).astype(o_ref.dtype)


def rmsnorm_cast(x, g, *, col_block, width, tm):
    M = x.shape[0]
    return pl.pallas_call(
        _rmsnorm_kernel,
        grid=(M // tm,),
        in_specs=[pl.BlockSpec((tm, width), lambda i: (i, col_block)),
                  pl.BlockSpec((1, width), lambda i: (0, 0))],
        out_specs=pl.BlockSpec((tm, width), lambda i: (i, 0)),
        out_shape=jax.ShapeDtypeStruct((M, width), BF16),
        compiler_params=_cparams("parallel"),
        name="rmsnorm_cast",
    )(x, g.reshape(1, width).astype(F32))


def _resid_norm_kernel(h_ref, y_ref, gpost_ref, gpre_ref, ho_ref, xn_ref, *, coef):
    h = h_ref[...] + coef * _rms(y_ref[...], gpost_ref[...])
    ho_ref[...] = h
    xn_ref[...] = _rms(h, gpre_ref[...]).astype(xn_ref.dtype)


def _resid_kernel(h_ref, y_ref, gpost_ref, ho_ref, *, coef):
    ho_ref[...] = h_ref[...] + coef * _rms(y_ref[...], gpost_ref[...])


def resid_norm(h, y, g_post, g_pre, *, coef, tm):
    M, D = h.shape
    row = pl.BlockSpec((tm, D), lambda i: (i, 0))
    gain = pl.BlockSpec((1, D), lambda i: (0, 0))
    if g_pre is None:
        return pl.pallas_call(
            functools.partial(_resid_kernel, coef=coef),
            grid=(M // tm,), in_specs=[row, row, gain], out_specs=row,
            out_shape=jax.ShapeDtypeStruct((M, D), F32),
            input_output_aliases={0: 0},
            compiler_params=_cparams("parallel"), name="resid",
        )(h, y, g_post.reshape(1, D)), None
    return pl.pallas_call(
        functools.partial(_resid_norm_kernel, coef=coef),
        grid=(M // tm,), in_specs=[row, row, gain, gain], out_specs=[row, row],
        out_shape=[jax.ShapeDtypeStruct((M, D), F32), jax.ShapeDtypeStruct((M, D), BF16)],
        input_output_aliases={0: 0},
        compiler_params=_cparams("parallel"), name="resid_norm",
    )(h, y, g_post.reshape(1, D), g_pre.reshape(1, D))


def _mm_kernel(x_ref, w_ref, o_ref, *scratch, nk, act):
    part = jnp.dot(x_ref[...], w_ref[...], preferred_element_type=F32)

    def finish(acc):
        if act == "sigmoid":
            acc = jax.nn.sigmoid(acc)
        o_ref[...] = acc.astype(o_ref.dtype)

    if nk == 1:
        finish(part)
        return
    acc_ref = scratch[0]
    k = pl.program_id(2)

    @pl.when(k == 0)
    def _():
        acc_ref[...] = part

    @pl.when(k > 0)
    def _():
        acc_ref[...] += part

    @pl.when(k == nk - 1)
    def _():
        finish(acc_ref[...])


def matmul(x, w, layer, *, tm, tn, tk, out_dtype, act=None, name="matmul"):
    M, K = x.shape
    N = w.shape[2]
    nk = K // tk
    scratch = [pltpu.VMEM((tm, tn), F32)] if nk > 1 else []
    return pl.pallas_call(
        functools.partial(_mm_kernel, nk=nk, act=act),
        grid=(M // tm, N // tn, nk),
        in_specs=[pl.BlockSpec((tm, tk), lambda i, j, k: (i, k)),
                  pl.BlockSpec((None, tk, tn), lambda i, j, k: (layer, k, j))],
        out_specs=pl.BlockSpec((tm, tn), lambda i, j, k: (i, j)),
        out_shape=jax.ShapeDtypeStruct((M, N), out_dtype),
        scratch_shapes=scratch,
        compiler_params=_cparams("parallel", "parallel", "arbitrary"),
        name=name,
    )(x, w)


def _mm_swiglu_kernel(x_ref, wa_ref, wb_ref, o_ref):
    x = x_ref[...]
    a = jnp.dot(x, wa_ref[...], preferred_element_type=F32)
    b = jnp.dot(x, wb_ref[...], preferred_element_type=F32)
    o_ref[...] = (jax.nn.silu(a) * b).astype(o_ref.dtype)


def matmul_swiglu(x, w13, layer, *, tm, tn):
    M, K = x.shape
    N = w13.shape[2] // 2
    nj = N // tn
    return pl.pallas_call(
        _mm_swiglu_kernel,
        grid=(M // tm, nj),
        in_specs=[pl.BlockSpec((tm, K), lambda i, j: (i, 0)),
                  pl.BlockSpec((None, K, tn), lambda i, j: (layer, 0, j)),
                  pl.BlockSpec((None, K, tn), lambda i, j: (layer, 0, nj + j))],
        out_specs=pl.BlockSpec((tm, tn), lambda i, j: (i, j)),
        out_shape=jax.ShapeDtypeStruct((M, N), BF16),
        compiler_params=_cparams("parallel", "parallel"),
        name="ffn_up_swiglu",
    )(x, w13, w13)


def _rope_block(a, c, s):
    return a * c + pltpu.roll(a, 64, axis=1) * s


def _mm_qrope_kernel(x_ref, w_ref, c_ref, s_ref, o_ref, *, heads, scale):
    acc = jnp.dot(x_ref[...], w_ref[...], preferred_element_type=F32) * scale
    c = c_ref[...]
    s = s_ref[...]
    for hb in range(heads):
        lo = hb * MLA_QPAD
        o_ref[:, lo:lo + 128] = acc[:, lo:lo + 128].astype(o_ref.dtype)
        o_ref[:, lo + 128:lo + 256] = _rope_block(acc[:, lo + 128:lo + 256], c, s).astype(o_ref.dtype)


def matmul_qrope(x, w, layer, cos_t, sin_t, *, tm, heads_per_tile, scale):
    M, K = x.shape
    N = w.shape[2]
    tn = heads_per_tile * MLA_QPAD
    tab = pl.BlockSpec((tm, 128), lambda i, j: (i, 0))
    return pl.pallas_call(
        functools.partial(_mm_qrope_kernel, heads=heads_per_tile, scale=scale),
        grid=(M // tm, N // tn),
        in_specs=[pl.BlockSpec((tm, K), lambda i, j: (i, 0)),
                  pl.BlockSpec((None, K, tn), lambda i, j: (layer, 0, j)), tab, tab],
        out_specs=pl.BlockSpec((tm, tn), lambda i, j: (i, j)),
        out_shape=jax.ShapeDtypeStruct((M, N), BF16),
        compiler_params=_cparams("parallel", "parallel"),
        name="mla_q_proj_rope",
    )(x, w, cos_t, sin_t)


def _krope_kernel(x_ref, c_ref, s_ref, o_ref):
    o_ref[...] = _rope_block(x_ref[...], c_ref[...], s_ref[...]).astype(o_ref.dtype)


def k_rope(p, cos_t, sin_t, *, tm):
    M = p.shape[0]
    tab = pl.BlockSpec((tm, 128), lambda i: (i, 0))
    return pl.pallas_call(
        _krope_kernel,
        grid=(M // tm,),
        in_specs=[pl.BlockSpec((tm, 128), lambda i: (i, P_KR // 128)), tab, tab],
        out_specs=tab,
        out_shape=jax.ShapeDtypeStruct((M, 128), BF16),
        compiler_params=_cparams("parallel"),
        name="mla_k_rope",
    )(p, cos_t, sin_t)


def _merge_kernel(o_ref, hs_ref, dn_ref, wo_ref, wv_ref, wg_ref, wd_ref,
                  g0_ref, g1_ref, g2_ref, out_ref):
    hs = hs_ref[...]
    y_mla = jnp.dot(o_ref[...], wo_ref[...], preferred_element_type=F32)
    val = jnp.dot(hs, wv_ref[...], preferred_element_type=F32)
    gate = jnp.dot(hs, wg_ref[...], preferred_element_type=F32)
    y_dn = jnp.dot(dn_ref[...], wd_ref[...], preferred_element_type=F32)
    merged = (g0_ref[...].astype(F32) * y_mla
              + g1_ref[...].astype(F32) * (val * jax.nn.sigmoid(gate))
              + g2_ref[...].astype(F32) * y_dn)
    out_ref[...] = merged.astype(out_ref.dtype)


def merge_branches(o_mla, h_s5, o_dn, w_o, w_glu, w_dn_o, gates, layer, *, tm, tn):
    M = o_mla.shape[0]
    D = w_o.shape[2]
    nj = D // tn

    def rows(width):
        return pl.BlockSpec((tm, width), lambda i, j: (i, 0))

    def cols(kdim, off):
        return pl.BlockSpec((None, kdim, tn), lambda i, j: (layer, 0, off * nj + j))

    def gate(b):
        return pl.BlockSpec((tm, tn), lambda i, j: (i, b * nj + j))

    return pl.pallas_call(
        _merge_kernel,
        grid=(M // tm, nj),
        in_specs=[rows(MLA_WIDTH), rows(S5_WIDTH), rows(DN_WIDTH),
                  cols(MLA_WIDTH, 0), cols(S5_WIDTH, 0), cols(S5_WIDTH, 1), cols(DN_WIDTH, 0),
                  gate(0), gate(1), gate(2)],
        out_specs=pl.BlockSpec((tm, tn), lambda i, j: (i, j)),
        out_shape=jax.ShapeDtypeStruct((M, D), BF16),
        compiler_params=_cparams("parallel", "parallel"),
        name="merge_branches",
    )(o_mla, h_s5, o_dn, w_o, w_glu, w_glu, w_dn_o, gates, gates, gates)


_NEG = -1e30


def _chunk_id(pos):
    return (pos + (CHUNK - N_META)) >> 6


ATTN_HEADS_PER_STEP = 4


def _attn_kernel(q_ref, kn_ref, kr_ref, v_ref, o_ref, *, tq, seq_pad):
    G = ATTN_HEADS_PER_STEP
    tk = tq
    tkm = tq + 128
    q0 = pl.program_id(2) * tq
    qs = [q_ref[:, g * MLA_QPAD:(g + 1) * MLA_QPAD] for g in range(G)]
    full_end = (_chunk_id(q0) + 1) * CHUNK - (CHUNK - N_META)
    n_full = full_end // tk

    def step(ks, width, carry, mask):
        ms, accs = carry
        k_rope = kr_ref[pl.ds(ks, width), :]
        ones = jnp.ones((width, 128), BF16)
        ss = []
        for g in range(G):
            k = jnp.concatenate([kn_ref[pl.ds(ks, width), g * 128:(g + 1) * 128], k_rope], axis=1)
            ss.append(lax.dot_general(qs[g], k, (((1,), (1,)), ((), ())), preferred_element_type=F32))
        if mask is not None:
            ss = [jnp.where(mask, s, _NEG) for s in ss]
        m_new = [jnp.maximum(m, jnp.max(s, axis=-1, keepdims=True)) for m, s in zip(ms, ss)]
        ps = [jnp.exp2(s - m).astype(BF16) for s, m in zip(ss, m_new)]
        out = []
        for g in range(G):
            v1 = jnp.concatenate([v_ref[pl.ds(ks, width), g * 128:(g + 1) * 128], ones], axis=1)
            out.append(jnp.exp2(ms[g] - m_new[g]) * accs[g]
                       + jnp.dot(ps[g], v1, preferred_element_type=F32))
        return m_new, out

    def full_step(j, carry):
        return step(pl.multiple_of(j * tk, tk), tk, carry, None)

    init = ([jnp.full((tq, 1), _NEG, F32)] * G, [jnp.zeros((tq, 2 * MLA_V), F32)] * G)
    carry = lax.fori_loop(0, n_full, full_step, init)

    first = n_full * tk
    ks = pl.multiple_of(jnp.minimum(first, seq_pad - tkm), 128)
    kpos = ks + lax.broadcasted_iota(jnp.int32, (1, tkm), 1)
    cq = _chunk_id(q0 + lax.broadcasted_iota(jnp.int32, (tq, 1), 0))
    mask = jnp.logical_and(cq >= _chunk_id(kpos), kpos >= first)
    _, accs = step(ks, tkm, carry, mask)
    for g in range(G):
        o_ref[:, g * MLA_V:(g + 1) * MLA_V] = (
            accs[g][:, :MLA_V] / accs[g][:, MLA_V:]).astype(o_ref.dtype)


def mla_attention(q, kv, kr, *, batch, seq_pad, tq):
    M = q.shape[0]
    nq = seq_pad // tq
    G = ATTN_HEADS_PER_STEP
    ng = MLA_HEADS // G
    assert tq + 128 <= seq_pad and tq > CHUNK
    return pl.pallas_call(
        functools.partial(_attn_kernel, tq=tq, seq_pad=seq_pad),
        grid=(batch, ng, nq),
        in_specs=[pl.BlockSpec((tq, G * MLA_QPAD), lambda b, h, i: (b * nq + i, h)),
                  pl.BlockSpec((seq_pad, G * 128), lambda b, h, i: (b, h)),
                  pl.BlockSpec((seq_pad, 128), lambda b, h, i: (b, 0)),
                  pl.BlockSpec((seq_pad, G * 128), lambda b, h, i: (b, ng + h))],
        out_specs=pl.BlockSpec((tq, G * MLA_V), lambda b, h, i: (b * nq + i, h)),
        out_shape=jax.ShapeDtypeStruct((M, MLA_WIDTH), BF16),
        compiler_params=_cparams("parallel", "parallel", "parallel"),
        name="mla_attention",
    )(q, kv, kr, kv)


def _s5_kernel(u_ref, br_ref, bi_ref, cr_ref, ci_ref, d_ref, ast_ref, pw_ref, o_ref,
               xr_s, xi_s, car_s, *, tt):
    n = S5_GPB * S5_STATE

    @pl.when(pl.program_id(2) == 0)
    def _():
        car_s[...] = jnp.zeros_like(car_s)

    u = u_ref[...]
    ub = u.astype(BF16)
    xr = jnp.dot(ub, br_ref[...], preferred_element_type=F32).reshape(tt // 8, 8, n)
    xi = jnp.dot(ub, bi_ref[...], preferred_element_type=F32).reshape(tt // 8, 8, n)
    for si in range(3):
        ar = ast_ref[2 * si]
        ai = ast_ref[2 * si + 1]
        sr = pltpu.roll(xr, 1 << si, axis=1)
        sm = pltpu.roll(xi, 1 << si, axis=1)
        xr, xi = xr + (ar * sr - ai * sm), xi + (ar * sm + ai * sr)
    xr_s[...] = xr.reshape(tt, n)
    xi_s[...] = xi.reshape(tt, n)
    pr = pw_ref[0]
    pi = pw_ref[1]

    def body(i, carry):
        cr, ci = carry
        sl = pl.ds(pl.multiple_of(i * 8, 8), 8)
        a = xr_s[sl, :] + (pr * cr - pi * ci)
        b = xi_s[sl, :] + (pr * ci + pi * cr)
        xr_s[sl, :] = a
        xi_s[sl, :] = b
        return (jnp.broadcast_to(a[7:8, :], (8, n)), jnp.broadcast_to(b[7:8, :], (8, n)))

    cr, ci = lax.fori_loop(0, tt // 8, body, (car_s[0], car_s[1]))
    car_s[0] = cr
    car_s[1] = ci
    y = (jnp.dot(xr_s[...].astype(BF16), cr_ref[...], preferred_element_type=F32)
         - jnp.dot(xi_s[...].astype(BF16), ci_ref[...], preferred_element_type=F32)
         + d_ref[...] * u)
    o_ref[...] = jax.nn.gelu(y).astype(o_ref.dtype)


def s5_scan(p, s5c, *, batch, seq_pad, tt):
    M = p.shape[0]
    nt = seq_pad // tt
    cw = S5_GPB * S5_GROUP
    n = S5_GPB * S5_STATE
    ublk = P_S5 // cw
    return pl.pallas_call(
        functools.partial(_s5_kernel, tt=tt),
        grid=(batch, S5_GB, nt),
        in_specs=[pl.BlockSpec((tt, cw), lambda b, g, t: (b * nt + t, ublk + g)),
                  pl.BlockSpec((None, cw, n), lambda b, g, t: (g, 0, 0)),
                  pl.BlockSpec((None, cw, n), lambda b, g, t: (g, 0, 0)),
                  pl.BlockSpec((None, n, cw), lambda b, g, t: (g, 0, 0)),
                  pl.BlockSpec((None, n, cw), lambda b, g, t: (g, 0, 0)),
                  pl.BlockSpec((None, 1, cw), lambda b, g, t: (g, 0, 0)),
                  pl.BlockSpec((None, 6, 8, n), lambda b, g, t: (g, 0, 0, 0)),
                  pl.BlockSpec((None, 2, 8, n), lambda b, g, t: (g, 0, 0, 0))],
        out_specs=pl.BlockSpec((tt, cw), lambda b, g, t: (b * nt + t, g)),
        out_shape=jax.ShapeDtypeStruct((M, S5_WIDTH), BF16),
        scratch_shapes=[pltpu.VMEM((tt, n), F32), pltpu.VMEM((tt, n), F32),
                        pltpu.VMEM((2, 8, n), F32)],
        compiler_params=_cparams("parallel", "parallel", "arbitrary"),
        name="s5_scan",
    )(p, s5c["br"], s5c["bi"], s5c["cr"], s5c["ci"], s5c["d"], s5c["ast"], s5c["pw"])


def s5_constants(a_re, a_im, log_dt, b_re, b_im, c_re, c_im, d):
    ar, ai = a_re.astype(F32), a_im.astype(F32)
    delta = jnp.exp(log_dt.astype(F32))[:, None]
    mag = jnp.exp(ar * delta)
    abar_r, abar_i = mag * jnp.cos(ai * delta), mag * jnp.sin(ai * delta)
    den = ar * ar + ai * ai
    zr = ((abar_r - 1.0) * ar + abar_i * ai) / den
    zi = (abar_i * ar - (abar_r - 1.0) * ai) / den
    br, bi = b_re.astype(F32), b_im.astype(F32)
    bbar_r = zr[..., None] * br - zi[..., None] * bi
    bbar_i = zr[..., None] * bi + zi[..., None] * br

    eye = jnp.eye(S5_GPB, dtype=F32)

    def in_blockdiag(m):
        m = m.reshape(S5_GB, S5_GPB, S5_STATE, S5_GROUP)
        bd = jnp.einsum("bgpc,gh->bgchp", m, eye)
        return bd.reshape(S5_GB, S5_GPB * S5_GROUP, S5_GPB * S5_STATE).astype(BF16)

    def out_blockdiag(m):
        m = m.reshape(S5_GB, S5_GPB, S5_GROUP, S5_STATE)
        bd = jnp.einsum("bgcp,gh->bgphc", m, eye)
        return bd.reshape(S5_GB, S5_GPB * S5_STATE, S5_GPB * S5_GROUP).astype(BF16)

    def cmul(x, y):
        return x[0] * y[0] - x[1] * y[1], x[0] * y[1] + x[1] * y[0]

    n = S5_GPB * S5_STATE
    a1 = (abar_r.reshape(S5_GB, n), abar_i.reshape(S5_GB, n))
    a2 = cmul(a1, a1)
    a4 = cmul(a2, a2)
    row = jnp.arange(8)[None, :, None]
    ast = []
    for s, a in ((1, a1), (2, a2), (4, a4)):
        for part in a:
            ast.append(jnp.where(row >= s, part[:, None, :], 0.0))
    ast = jnp.stack(ast, axis=1)
    pows = [a1]
    for _ in range(7):
        pows.append(cmul(pows[-1], a1))
    pw = jnp.stack([jnp.stack([p[0] for p in pows], axis=1),
                    jnp.stack([p[1] for p in pows], axis=1)], axis=1)
    return {
        "br": in_blockdiag(bbar_r), "bi": in_blockdiag(bbar_i),
        "cr": out_blockdiag(c_re.astype(F32)), "ci": out_blockdiag(c_im.astype(F32)),
        "d": d.astype(F32).reshape(S5_GB, 1, S5_GPB * S5_GROUP),
        "ast": ast, "pw": pw,
    }


def _dnconv_kernel(x_ref, w_ref, o_ref, buf, *, tt):
    @pl.when(pl.program_id(2) == 0)
    def _():
        buf[0:8, :] = jnp.zeros((8, buf.shape[1]), F32)

    buf[8:8 + tt, :] = x_ref[...]
    acc = buf[8:8 + tt, :] * w_ref[3:4, :]
    for j in range(DN_CONV - 1):
        lo = 8 - (DN_CONV - 1) + j
        acc = acc + buf[lo:lo + tt, :] * w_ref[j:j + 1, :]
    o_ref[...] = jax.nn.silu(acc)
    buf[0:8, :] = buf[tt:tt + 8, :]


def dn_conv(p, conv_w, *, batch, seq_pad, tt, tc):
    M = p.shape[0]
    nt = seq_pad // tt
    width = conv_w.shape[1]
    cblk = P_DNQKV // tc
    return pl.pallas_call(
        functools.partial(_dnconv_kernel, tt=tt),
        grid=(batch, width // tc, nt),
        in_specs=[pl.BlockSpec((tt, tc), lambda b, c, t: (b * nt + t, cblk + c)),
                  pl.BlockSpec((DN_CONV, tc), lambda b, c, t: (0, c))],
        out_specs=pl.BlockSpec((tt, tc), lambda b, c, t: (b * nt + t, c)),
        out_shape=jax.ShapeDtypeStruct((M, width), F32),
        scratch_shapes=[pltpu.VMEM((tt + 8, tc), F32)],
        compiler_params=_cparams("parallel", "parallel", "arbitrary"),
        name="dn_conv_silu",
    )(p, conv_w.astype(F32))


def _dn_gates_kernel(ab_ref, alog_ref, dtb_ref, gcum_ref, beta_ref):
    ab = ab_ref[...]
    g = -jnp.exp(alog_ref[...]) * jax.nn.softplus(ab[:DN_HEADS] + dtb_ref[...])
    beta_ref[...] = jax.nn.sigmoid(ab[DN_HEADS:])
    lane = lax.broadcasted_iota(jnp.int32, g.shape, 1) % CHUNK
    s = 1
    while s < CHUNK:
        g = g + jnp.where(lane >= s, pltpu.roll(g, s, axis=1), 0.0)
        s *= 2
    gcum_ref[...] = g


def dn_gates(ab_t, a_log, dt_bias):
    B, _, Lp = ab_t.shape
    col = pl.BlockSpec((DN_HEADS, 1), lambda b: (0, 0))
    out = pl.BlockSpec((None, DN_HEADS, Lp), lambda b: (b, 0, 0))
    return pl.pallas_call(
        _dn_gates_kernel,
        grid=(B,),
        in_specs=[pl.BlockSpec((None, 2 * DN_HEADS, Lp), lambda b: (b, 0, 0)), col, col],
        out_specs=[out, out],
        out_shape=[jax.ShapeDtypeStruct((B, DN_HEADS, Lp), F32)] * 2,
        compiler_params=_cparams("parallel"),
        name="dn_gates",
    )(ab_t, a_log.astype(F32).reshape(DN_HEADS, 1), dt_bias.astype(F32).reshape(DN_HEADS, 1))


def _dot(a, b):
    return jnp.dot(a, b, preferred_element_type=F32)


def _dot_nt(a, b):
    return lax.dot_general(a, b, (((1,), (1,)), ((), ())), preferred_element_type=F32)


def _dot_tn(a, b):
    return lax.dot_general(a, b, (((0,), (0,)), ((), ())), preferred_element_type=F32)


def _l2(x):
    return x * lax.rsqrt(jnp.sum(x * x, axis=-1, keepdims=True) + EPS)


def _dn_prep_kernel(q_ref, k_ref, v_ref, bcol_ref, gcol_ref, grow_ref,
                    u_ref, w_ref, qg_ref, kd_ref, qk_ref, dl_ref, *, n_chunks, unroll):
    C = CHUNK
    ii = lax.broadcasted_iota(jnp.int32, (C, C), 0)
    jj = lax.broadcasted_iota(jnp.int32, (C, C), 1)
    tri = ii >= jj
    strict = ii > jj
    eye = (ii == jj).astype(F32)

    def body(i, carry):
        U = range(unroll)
        cs = [i * unroll + j for j in U]
        sls = [pl.ds(pl.multiple_of(c * C, C), C) for c in cs]
        q = [_l2(q_ref[sl, :]) * (DN_DK ** -0.5) for sl in sls]
        k = [_l2(k_ref[sl, :]) for sl in sls]
        beta = [bcol_ref[sl, :] for sl in sls]
        gc = [gcol_ref[sl, :] for sl in sls]
        decay = [jnp.exp(jnp.where(tri, gc[j] - grow_ref[pl.ds(cs[j], 1), :], -jnp.inf)) for j in U]
        kb = [k[j] * beta[j] for j in U]
        a_mat = [jnp.where(strict, _dot_nt(kb[j], k[j]) * decay[j], 0.0) for j in U]
        t_inv = [eye - a for a in a_mat]
        pw = a_mat
        for _ in range(5):
            pw = [_dot(x, x) for x in pw]
            t_inv = [t + _dot(t, x) for t, x in zip(t_inv, pw)]
        u = [_dot(t_inv[j], v_ref[sls[j], :] * beta[j]) for j in U]
        w = [_dot(t_inv[j], kb[j] * jnp.exp(gc[j])) for j in U]
        qk = [_dot_nt(q[j].astype(BF16), k[j].astype(BF16)) * decay[j] for j in U]
        for j in U:
            sl = sls[j]
            u_ref[sl, :] = u[j]
            w_ref[sl, :] = w[j].astype(w_ref.dtype)
            qk_ref[sl, :] = qk[j].astype(qk_ref.dtype)
            qg_ref[sl, :] = (q[j] * jnp.exp(gc[j])).astype(qg_ref.dtype)
            g_last = gc[j][C - 1:C, :]
            kd_ref[sl, :] = (k[j] * jnp.exp(g_last - gc[j])).astype(kd_ref.dtype)
            dl_ref[pl.ds(cs[j], 1), :] = jnp.broadcast_to(jnp.exp(g_last), (1, dl_ref.shape[1]))
        return carry

    lax.fori_loop(0, n_chunks // unroll, body, 0)


def dn_prep(qkv, beta_col, gcum_col, gcum_row, *, batch, seq_pad):
    M = qkv.shape[0]
    H = DN_HEADS
    nc = seq_pad // CHUNK

    def head(off):
        return pl.BlockSpec((seq_pad, 128), lambda b, h: (b, off + h))

    colspec = pl.BlockSpec((None, None, seq_pad, 1), lambda b, h: (b, h, 0, 0))
    wide = jax.ShapeDtypeStruct((M, DN_WIDTH), BF16)
    return pl.pallas_call(
        functools.partial(_dn_prep_kernel, n_chunks=nc, unroll=_pick(nc, (6, 4, 3, 2, 1))),
        grid=(batch, H),
        in_specs=[head(0), head(H), head(2 * H), colspec, colspec,
                  pl.BlockSpec((None, None, nc, CHUNK), lambda b, h: (b, h, 0, 0))],
        out_specs=[head(0), head(0), head(0), head(0),
                   pl.BlockSpec((None, None, seq_pad, CHUNK), lambda b, h: (b, h, 0, 0)),
                   pl.BlockSpec((None, None, nc, 128), lambda b, h: (b, h, 0, 0))],
        out_shape=[jax.ShapeDtypeStruct((M, DN_WIDTH), F32), wide, wide, wide,
                   jax.ShapeDtypeStruct((batch, H, seq_pad, CHUNK), BF16),
                   jax.ShapeDtypeStruct((batch, H, nc, 128), F32)],
        compiler_params=_cparams("parallel", "parallel"),
        name="dn_prep",
    )(qkv, qkv, qkv, beta_col, gcum_col, gcum_row)


def _dn_scan_kernel(u_ref, w_ref, qg_ref, kd_ref, qk_ref, dl_ref, z_ref, gn_ref, o_ref, s_ref,
                    *, chunks_per_tile):
    C = CHUNK

    @pl.when(pl.program_id(1) == 0)
    def _():
        s_ref[...] = jnp.zeros_like(s_ref)

    gn = gn_ref[...]

    def body(c, carry):
        sl = pl.ds(pl.multiple_of(c * C, C), C)
        H = range(DN_HEADS)
        hs = [slice(j * DN_DV, (j + 1) * DN_DV) for j in H]
        S = [s_ref[j] for j in H]
        Sb = [s.astype(BF16) for s in S]
        v_new = [u_ref[sl, hs[j]] - _dot(w_ref[sl, hs[j]], Sb[j]) for j in H]
        o_inter = [_dot(qg_ref[sl, hs[j]], Sb[j]) for j in H]
        vb = [v.astype(BF16) for v in v_new]
        o = [o_inter[j] + _dot(qk_ref[j, sl, :], vb[j]) for j in H]
        s_upd = [_dot_tn(kd_ref[sl, hs[j]], vb[j]) for j in H]
        for j in H:
            s_ref[j] = S[j] * dl_ref[j, pl.ds(c, 1), :] + s_upd[j]
            o_ref[sl, hs[j]] = (_rms(o[j], gn) * jax.nn.silu(z_ref[sl, hs[j]])).astype(o_ref.dtype)
        return carry

    lax.fori_loop(0, chunks_per_tile, body, 0)


def dn_scan(u, w, qg, kd, qk, dl, p, out_norm_g, *, batch, seq_pad):
    M = p.shape[0]
    H = DN_HEADS
    nc = seq_pad // CHUNK
    cpt = _pick(nc, (11, 6, 4, 3, 2, 1))
    nt = nc // cpt
    tt = cpt * CHUNK
    dl = dl.reshape(batch, H, nt, cpt, 128)

    def rows(col_block=0):
        return pl.BlockSpec((tt, DN_WIDTH), lambda b, t: (b * nt + t, col_block))

    return pl.pallas_call(
        functools.partial(_dn_scan_kernel, chunks_per_tile=cpt),
        grid=(batch, nt),
        in_specs=[rows(), rows(), rows(), rows(),
                  pl.BlockSpec((None, H, tt, CHUNK), lambda b, t: (b, 0, t, 0)),
                  pl.BlockSpec((None, H, None, cpt, 128), lambda b, t: (b, 0, t, 0, 0)),
                  rows(P_DNZ // DN_WIDTH),
                  pl.BlockSpec((1, DN_DV), lambda b, t: (0, 0))],
        out_specs=rows(),
        out_shape=jax.ShapeDtypeStruct((M, DN_WIDTH), BF16),
        scratch_shapes=[pltpu.VMEM((H, DN_DK, DN_DV), F32)],
        compiler_params=_cparams("parallel", "arbitrary"),
        name="dn_scan",
    )(u, w, qg, kd, qk, dl, p, out_norm_g.astype(F32).reshape(1, DN_DV))


def _rot_cols(w):
    half = w.shape[-1] // 2
    return jnp.concatenate([-w[..., half:], w[..., :half]], axis=-1)


def _prep_mixer(w_in, mla_w_uq, mla_w_ukv):
    depth = w_in.shape[0]
    kr = w_in[..., OFF_KR:OFF_S5]
    w_small = jnp.concatenate([
        w_in[..., OFF_Q:OFF_KV], w_in[..., OFF_DN_Z:OFF_DN_A], w_in[..., OFF_S5:OFF_DN_Z],
        w_in[..., OFF_KV:OFF_KR], kr, _rot_cols(kr), w_in[..., OFF_DN_A:OFF_GATE],
        jnp.zeros((depth, D_MODEL, 128 - 2 * DN_HEADS), w_in.dtype)], axis=-1).astype(BF16)
    w_gate = w_in[..., OFF_GATE:].astype(BF16)
    uq = mla_w_uq.reshape(depth, MLA_Q_LORA, MLA_HEADS, MLA_QK)
    w_uq = jnp.concatenate([uq, _rot_cols(uq[..., MLA_NOPE:])], axis=-1)
    w_uq = w_uq.reshape(depth, MLA_Q_LORA, MLA_HEADS * MLA_QPAD).astype(BF16)
    ukv = mla_w_ukv.reshape(depth, MLA_KV_LORA, MLA_HEADS, 2, 128)
    w_ukv = jnp.transpose(ukv, (0, 1, 3, 2, 4)).reshape(depth, MLA_KV_LORA, 2 * MLA_HEADS * 128)
    return w_small, w_gate, w_uq, w_ukv.astype(BF16)


def _prep_ffn(w13, w2):
    depth, d, _ = w13.shape
    pad = FFN_PAD - FFN_HIDDEN
    w13 = jnp.pad(w13.reshape(depth, d, 2, FFN_HIDDEN), ((0, 0), (0, 0), (0, 0), (0, pad)))
    w2 = jnp.pad(w2, ((0, 0), (0, pad), (0, 0)))
    return w13.astype(BF16).reshape(depth, d, 2 * FFN_PAD), w2.astype(BF16)


def _rope_tables(seq_pad, batch):
    inv = ROPE_THETA ** (-jnp.arange(0, MLA_ROPE, 2, dtype=F32) / MLA_ROPE)
    ang = jnp.arange(seq_pad, dtype=F32)[:, None] * inv[None, :]
    zeros = jnp.zeros((seq_pad, 128 - MLA_ROPE), F32)
    cos_t = jnp.concatenate([jnp.cos(ang), jnp.cos(ang), zeros], axis=1)
    sin_t = jnp.concatenate([jnp.sin(ang), jnp.sin(ang), zeros], axis=1)
    return jnp.tile(cos_t, (batch, 1)), jnp.tile(sin_t, (batch, 1))


def _ffn(xn, w13, w2, layer, *, tm):
    hid = matmul_swiglu(xn, w13, layer, tm=tm, tn=512)
    return matmul(hid, w2, layer, tm=tm, tn=1024, tk=FFN_PAD // 4, out_dtype=F32, name="ffn_down")


def kernel(x, meta_tokens, sandwich_g, ffn1_w13, ffn1_w2, w_in, mla_q_norm_g, mla_kv_norm_g, mla_w_uq, mla_w_ukv, mla_w_o, s5_a_re, s5_a_im, s5_log_dt, s5_b_re, s5_b_im, s5_c_re, s5_c_im, s5_d, s5_w_glu, dn_conv_w, dn_a_log, dn_dt_bias, dn_out_norm_g, dn_w_o, w_out, ffn2_w13, ffn2_w2):
    B, seq, D = x.shape
    depth = w_in.shape[0]
    L = N_META + seq
    assert (L - N_META) % CHUNK == 0, "pad keys are hidden by the chunk mask only on a chunk boundary"
    Lp = -(-L // 128) * 128
    M = B * Lp
    tm = _pick(M, (1056, 768, 384, 128))
    tr = _pick(M, (176, 128))
    tt = _pick(Lp, (528, 384, 128))
    ta = _pick(Lp, tuple(c for c in (384, 128) if c + 128 <= Lp))

    meta = jnp.broadcast_to(meta_tokens[None].astype(x.dtype), (B, N_META, D))
    h = jnp.concatenate([meta, x, jnp.zeros((B, Lp - L, D), x.dtype)], axis=1).reshape(M, D)
    cos_t, sin_t = _rope_tables(Lp, B)

    f1_w13, f1_w2 = _prep_ffn(ffn1_w13, ffn1_w2)
    f2_w13, f2_w2 = _prep_ffn(ffn2_w13, ffn2_w2)
    w_small, w_gate, w_uq, w_ukv = _prep_mixer(w_in, mla_w_uq, mla_w_ukv)
    w_mla_o, w_glu, w_dn_o, w_mix = (t.astype(BF16) for t in (mla_w_o, s5_w_glu, dn_w_o, w_out))
    q_scale = MLA_QK ** -0.5 * math.log2(math.e)

    xn = rmsnorm_cast(h, sandwich_g[0, 0], col_block=0, width=D, tm=tr)
    for l in range(depth):
        g = sandwich_g[l]
        y = _ffn(xn, f1_w13, f1_w2, l, tm=tm)
        h, xn = resid_norm(h, y, g[1], g[2], coef=0.5, tm=tr)

        p = matmul(xn, w_small, l, tm=tm, tn=768, tk=D, out_dtype=F32, name="in_proj_small")
        gates = matmul(xn, w_gate, l, tm=tm, tn=1024, tk=D, out_dtype=BF16, act="sigmoid",
                       name="in_proj_gates")

        qn = rmsnorm_cast(p, mla_q_norm_g[l], col_block=P_Q // MLA_Q_LORA, width=MLA_Q_LORA, tm=tm)
        kvn = rmsnorm_cast(p, mla_kv_norm_g[l], col_block=P_KV // MLA_KV_LORA, width=MLA_KV_LORA, tm=tm)
        q = matmul_qrope(qn, w_uq, l, cos_t, sin_t, tm=tm, heads_per_tile=4, scale=q_scale)
        kv = matmul(kvn, w_ukv, l, tm=tm, tn=1024, tk=MLA_KV_LORA, out_dtype=BF16, name="mla_kv_proj")
        kr = k_rope(p, cos_t, sin_t, tm=tm)
        o_mla = mla_attention(q, kv, kr, batch=B, seq_pad=Lp, tq=ta)

        s5c = s5_constants(s5_a_re[l], s5_a_im[l], s5_log_dt[l], s5_b_re[l], s5_b_im[l],
                           s5_c_re[l], s5_c_im[l], s5_d[l])
        h_s5 = s5_scan(p, s5c, batch=B, seq_pad=Lp, tt=tt)

        qkv = dn_conv(p, dn_conv_w[l], batch=B, seq_pad=Lp, tt=tt, tc=512)
        ab_t = jnp.transpose(p[:, P_AB:P_AB + 2 * DN_HEADS].reshape(B, Lp, 2 * DN_HEADS), (0, 2, 1))
        gcum, beta = dn_gates(ab_t, dn_a_log[l], dn_dt_bias[l])
        u, w, qg, kd, qk, dl = dn_prep(qkv, beta[..., None], gcum[..., None],
                                       gcum.reshape(B, DN_HEADS, Lp // CHUNK, CHUNK),
                                       batch=B, seq_pad=Lp)
        o_dn = dn_scan(u, w, qg, kd, qk, dl, p, dn_out_norm_g[l], batch=B, seq_pad=Lp)

        merged = merge_branches(o_mla, h_s5, o_dn, w_mla_o, w_glu, w_dn_o, gates, l, tm=tm, tn=512)
        mix = matmul(merged, w_mix, l, tm=tm, tn=1024, tk=D, out_dtype=F32, name="out_proj")
        h, xn = resid_norm(h, mix, g[3], g[4], coef=1.0, tm=tr)

        y = _ffn(xn, f2_w13, f2_w2, l, tm=tm)
        g_next = sandwich_g[l + 1, 0] if l + 1 < depth else None
        h, xn = resid_norm(h, y, g[5], g_next, coef=0.5, tm=tr)
    return h.reshape(B, Lp, D)[:, N_META:L]
```

```python
import functools
import math

import jax
import jax.numpy as jnp
from jax import lax
from jax.experimental import pallas as pl
from jax.experimental.pallas import tpu as pltpu

F32 = jnp.float32
BF16 = jnp.bfloat16

D_MODEL = 4096
CHUNK = 64
N_META = 16
EPS = 1e-6

MLA_HEADS = 16
MLA_Q_LORA = 1024
MLA_KV_LORA = 512
MLA_NOPE = 128
MLA_ROPE = 64
MLA_V = 128
MLA_QK = MLA_NOPE + MLA_ROPE
MLA_WIDTH = MLA_HEADS * MLA_V
MLA_QPAD = 256
ROPE_THETA = 10000.0

S5_WIDTH = 1024
S5_GROUP = 16
S5_GROUPS = S5_WIDTH // S5_GROUP
S5_STATE = 64
S5_GB = 4
S5_GPB = S5_GROUPS // S5_GB

DN_HEADS = 8
DN_DK = 128
DN_DV = 128
DN_QK = DN_HEADS * DN_DK
DN_WIDTH = DN_HEADS * DN_DV
DN_CONV = 4

FFN_HIDDEN = 11008
FFN_PAD = 11264

OFF_Q = 0
OFF_KV = OFF_Q + MLA_Q_LORA
OFF_KR = OFF_KV + MLA_KV_LORA
OFF_S5 = OFF_KR + MLA_ROPE
OFF_DN_QKV = OFF_S5 + S5_WIDTH
OFF_DN_Z = OFF_DN_QKV + 2 * DN_QK + DN_WIDTH
OFF_DN_A = OFF_DN_Z + DN_WIDTH
OFF_DN_B = OFF_DN_A + DN_HEADS
OFF_GATE = OFF_DN_B + DN_HEADS

P_Q = 0
P_DNZ = P_Q + MLA_Q_LORA
P_S5 = P_DNZ + DN_WIDTH
P_DNQKV = P_S5 + S5_WIDTH
P_KV = P_DNQKV + 3 * DN_QK
P_KR = P_KV + MLA_KV_LORA
P_AB = P_KR + 128
P_WIDTH = P_AB + 128

VMEM_LIMIT_BYTES = 56 * 1024 * 1024


def _cparams(*sem):
    return pltpu.CompilerParams(dimension_semantics=sem, vmem_limit_bytes=VMEM_LIMIT_BYTES)


def _pick(n, candidates):
    for c in candidates:
        if n % c == 0:
            return c
    raise ValueError(f"no tile in {candidates} divides {n}")


def _rms(x, g):
    return x * lax.rsqrt(jnp.mean(x * x, axis=-1, keepdims=True) + EPS) * g


def _rmsnorm_kernel(x_ref, g_ref, o_ref):
    o_ref[...] = _rms(x_ref[...], g_ref[...]).astype(o_ref.dtype)


def rmsnorm_cast(x, g, *, col_block, width, tm):
    M = x.shape[0]
    return pl.pallas_call(
        _rmsnorm_kernel,
        grid=(M // tm,),
        in_specs=[pl.BlockSpec((tm, width), lambda i: (i, col_block)),
                  pl.BlockSpec((1, width), lambda i: (0, 0))],
        out_specs=pl.BlockSpec((tm, width), lambda i: (i, 0)),
        out_shape=jax.ShapeDtypeStruct((M, width), BF16),
        compiler_params=_cparams("parallel"),
        name="rmsnorm_cast",
    )(x, g.reshape(1, width).astype(F32))


def _resid_norm_kernel(h_ref, y_ref, gpost_ref, gpre_ref, ho_ref, xn_ref, *, coef):
    h = h_ref[...] + coef * _rms(y_ref[...], gpost_ref[...])
    ho_ref[...] = h
    xn_ref[...] = _rms(h, gpre_ref[...]).astype(xn_ref.dtype)


def _resid_kernel(h_ref, y_ref, gpost_ref, ho_ref, *, coef):
    ho_ref[...] = h_ref[...] + coef * _rms(y_ref[...], gpost_ref[...])


def resid_norm(h, y, g_post, g_pre, *, coef, tm):
    M, D = h.shape
    row = pl.BlockSpec((tm, D), lambda i: (i, 0))
    gain = pl.BlockSpec((1, D), lambda i: (0, 0))
    if g_pre is None:
        return pl.pallas_call(
            functools.partial(_resid_kernel, coef=coef),
            grid=(M // tm,), in_specs=[row, row, gain], out_specs=row,
            out_shape=jax.ShapeDtypeStruct((M, D), F32),
            input_output_aliases={0: 0},
            compiler_params=_cparams("parallel"), name="resid",
        )(h, y, g_post.reshape(1, D)), None
    return pl.pallas_call(
        functools.partial(_resid_norm_kernel, coef=coef),
        grid=(M // tm,), in_specs=[row, row, gain, gain], out_specs=[row, row],
        out_shape=[jax.ShapeDtypeStruct((M, D), F32), jax.ShapeDtypeStruct((M, D), BF16)],
        input_output_aliases={0: 0},
        compiler_params=_cparams("parallel"), name="resid_norm",
    )(h, y, g_post.reshape(1, D), g_pre.reshape(1, D))


def _mm_kernel(x_ref, w_ref, o_ref, *scratch, nk, act):
    part = jnp.dot(x_ref[...], w_ref[...], preferred_element_type=F32)

    def finish(acc):
        if act == "sigmoid":
            acc = jax.nn.sigmoid(acc)
        o_ref[...] = acc.astype(o_ref.dtype)

    if nk == 1:
        finish(part)
        return
    acc_ref = scratch[0]
    k = pl.program_id(2)

    @pl.when(k == 0)
    def _():
        acc_ref[...] = part

    @pl.when(k > 0)
    def _():
        acc_ref[...] += part

    @pl.when(k == nk - 1)
    def _():
        finish(acc_ref[...])


def matmul(x, w, layer, *, tm, tn, tk, out_dtype, act=None, name="matmul"):
    M, K = x.shape
    N = w.shape[2]
    nk = K // tk
    scratch = [pltpu.VMEM((tm, tn), F32)] if nk > 1 else []
    return pl.pallas_call(
        functools.partial(_mm_kernel, nk=nk, act=act),
        grid=(M // tm, N // tn, nk),
        in_specs=[pl.BlockSpec((tm, tk), lambda i, j, k: (i, k)),
                  pl.BlockSpec((None, tk, tn), lambda i, j, k: (layer, k, j))],
        out_specs=pl.BlockSpec((tm, tn), lambda i, j, k: (i, j)),
        out_shape=jax.ShapeDtypeStruct((M, N), out_dtype),
        scratch_shapes=scratch,
        compiler_params=_cparams("parallel", "parallel", "arbitrary"),
        name=name,
    )(x, w)


def _mm_swiglu_kernel(x_ref, wa_ref, wb_ref, o_ref, *, n_valid):
    @pl.when(pl.program_id(1) < n_valid)
    def _():
        x = x_ref[...]
        a = jnp.dot(x, wa_ref[...].astype(BF16), preferred_element_type=F32)
        b = jnp.dot(x, wb_ref[...].astype(BF16), preferred_element_type=F32)
        o_ref[...] = (jax.nn.silu(a) * b).astype(o_ref.dtype)

    @pl.when(pl.program_id(1) >= n_valid)
    def _():
        o_ref[...] = jnp.zeros_like(o_ref)


def matmul_swiglu(x, w13, layer, *, tm, tn, n_out):
    M, K = x.shape
    nj = w13.shape[2] // 2 // tn
    return pl.pallas_call(
        functools.partial(_mm_swiglu_kernel, n_valid=nj),
        grid=(M // tm, n_out // tn),
        in_specs=[pl.BlockSpec((tm, K), lambda i, j: (i, 0)),
                  pl.BlockSpec((None, K, tn), lambda i, j: (layer, 0, jnp.minimum(j, nj - 1))),
                  pl.BlockSpec((None, K, tn), lambda i, j: (layer, 0, nj + jnp.minimum(j, nj - 1)))],
        out_specs=pl.BlockSpec((tm, tn), lambda i, j: (i, j)),
        out_shape=jax.ShapeDtypeStruct((M, n_out), BF16),
        compiler_params=_cparams("parallel", "parallel"),
        name="ffn_up_swiglu",
    )(x, w13, w13)


def _rope_block(a, c, s):
    return a * c + pltpu.roll(a, 64, axis=1) * s


def _mm_qrope_kernel(x_ref, w_ref, c_ref, s_ref, o_ref, *, heads, scale):
    acc = jnp.dot(x_ref[...], w_ref[...], preferred_element_type=F32) * scale
    c = c_ref[...]
    s = s_ref[...]
    for hb in range(heads):
        lo = hb * MLA_QPAD
        o_ref[:, lo:lo + 128] = acc[:, lo:lo + 128].astype(o_ref.dtype)
        o_ref[:, lo + 128:lo + 256] = _rope_block(acc[:, lo + 128:lo + 256], c, s).astype(o_ref.dtype)


def matmul_qrope(x, w, layer, cos_t, sin_t, *, tm, heads_per_tile, scale):
    M, K = x.shape
    N = w.shape[2]
    tn = heads_per_tile * MLA_QPAD
    tab = pl.BlockSpec((tm, 128), lambda i, j: (i, 0))
    return pl.pallas_call(
        functools.partial(_mm_qrope_kernel, heads=heads_per_tile, scale=scale),
        grid=(M // tm, N // tn),
        in_specs=[pl.BlockSpec((tm, K), lambda i, j: (i, 0)),
                  pl.BlockSpec((None, K, tn), lambda i, j: (layer, 0, j)), tab, tab],
        out_specs=pl.BlockSpec((tm, tn), lambda i, j: (i, j)),
        out_shape=jax.ShapeDtypeStruct((M, N), BF16),
        compiler_params=_cparams("parallel", "parallel"),
        name="mla_q_proj_rope",
    )(x, w, cos_t, sin_t)


def _krope_kernel(x_ref, c_ref, s_ref, o_ref):
    o_ref[...] = _rope_block(x_ref[...], c_ref[...], s_ref[...]).astype(o_ref.dtype)


def k_rope(p, cos_t, sin_t, *, tm):
    M = p.shape[0]
    tab = pl.BlockSpec((tm, 128), lambda i: (i, 0))
    return pl.pallas_call(
        _krope_kernel,
        grid=(M // tm,),
        in_specs=[pl.BlockSpec((tm, 128), lambda i: (i, P_KR // 128)), tab, tab],
        out_specs=tab,
        out_shape=jax.ShapeDtypeStruct((M, 128), BF16),
        compiler_params=_cparams("parallel"),
        name="mla_k_rope",
    )(p, cos_t, sin_t)


def _merge_kernel(o_ref, hs_ref, dn_ref, wo_ref, wv_ref, wg_ref, wd_ref,
                  g0_ref, g1_ref, g2_ref, out_ref):
    hs = hs_ref[...]
    y_mla = jnp.dot(o_ref[...], wo_ref[...], preferred_element_type=F32)
    val = jnp.dot(hs, wv_ref[...], preferred_element_type=F32)
    gate = jnp.dot(hs, wg_ref[...], preferred_element_type=F32)
    y_dn = jnp.dot(dn_ref[...], wd_ref[...], preferred_element_type=F32)
    merged = (g0_ref[...].astype(F32) * y_mla
              + g1_ref[...].astype(F32) * (val * jax.nn.sigmoid(gate))
              + g2_ref[...].astype(F32) * y_dn)
    out_ref[...] = merged.astype(out_ref.dtype)


def merge_branches(o_mla, h_s5, o_dn, w_o, w_glu, w_dn_o, gates, layer, *, tm, tn):
    M = o_mla.shape[0]
    D = w_o.shape[2]
    nj = D // tn

    def rows(width):
        return pl.BlockSpec((tm, width), lambda i, j: (i, 0))

    def cols(kdim, off):
        return pl.BlockSpec((None, kdim, tn), lambda i, j: (layer, 0, off * nj + j))

    def gate(b):
        return pl.BlockSpec((tm, tn), lambda i, j: (i, b * nj + j))

    return pl.pallas_call(
        _merge_kernel,
        grid=(M // tm, nj),
        in_specs=[rows(MLA_WIDTH), rows(S5_WIDTH), rows(DN_WIDTH),
                  cols(MLA_WIDTH, 0), cols(S5_WIDTH, 0), cols(S5_WIDTH, 1), cols(DN_WIDTH, 0),
                  gate(0), gate(1), gate(2)],
        out_specs=pl.BlockSpec((tm, tn), lambda i, j: (i, j)),
        out_shape=jax.ShapeDtypeStruct((M, D), BF16),
        compiler_params=_cparams("parallel", "parallel"),
        name="merge_branches",
    )(o_mla, h_s5, o_dn, w_o, w_glu, w_glu, w_dn_o, gates, gates, gates)


_NEG = -1e30


def _chunk_id(pos):
    return (pos + (CHUNK - N_META)) >> 6


ATTN_HEADS_PER_STEP = 4


def _attn_kernel(q_ref, kn_ref, kr_ref, v_ref, o_ref, *, tq, seq_pad):
    G = ATTN_HEADS_PER_STEP
    tk = tq
    tkm = tq + 128
    q0 = pl.program_id(2) * tq
    qs = [q_ref[:, g * MLA_QPAD:(g + 1) * MLA_QPAD] for g in range(G)]
    full_end = (_chunk_id(q0) + 1) * CHUNK - (CHUNK - N_META)
    n_full = full_end // tk

    def step(ks, width, carry, mask):
        ms, accs = carry
        k_rope = kr_ref[pl.ds(ks, width), :]
        ones = jnp.ones((width, 128), BF16)
        ss = []
        for g in range(G):
            k = jnp.concatenate([kn_ref[pl.ds(ks, width), g * 128:(g + 1) * 128], k_rope], axis=1)
            ss.append(lax.dot_general(qs[g], k, (((1,), (1,)), ((), ())), preferred_element_type=F32))
        if mask is not None:
            ss = [jnp.where(mask, s, _NEG) for s in ss]
        m_new = [jnp.maximum(m, jnp.max(s, axis=-1, keepdims=True)) for m, s in zip(ms, ss)]
        ps = [jnp.exp2(s - m).astype(BF16) for s, m in zip(ss, m_new)]
        out = []
        for g in range(G):
            v1 = jnp.concatenate([v_ref[pl.ds(ks, width), g * 128:(g + 1) * 128], ones], axis=1)
            out.append(jnp.exp2(ms[g] - m_new[g]) * accs[g]
                       + jnp.dot(ps[g], v1, preferred_element_type=F32))
        return m_new, out

    def full_step(j, carry):
        return step(pl.multiple_of(j * tk, tk), tk, carry, None)

    init = ([jnp.full((tq, 1), _NEG, F32)] * G, [jnp.zeros((tq, 2 * MLA_V), F32)] * G)
    carry = lax.fori_loop(0, n_full, full_step, init)

    first = n_full * tk
    ks = pl.multiple_of(jnp.minimum(first, seq_pad - tkm), 128)
    kpos = ks + lax.broadcasted_iota(jnp.int32, (1, tkm), 1)
    cq = _chunk_id(q0 + lax.broadcasted_iota(jnp.int32, (tq, 1), 0))
    mask = jnp.logical_and(cq >= _chunk_id(kpos), kpos >= first)
    _, accs = step(ks, tkm, carry, mask)
    for g in range(G):
        o_ref[:, g * MLA_V:(g + 1) * MLA_V] = (
            accs[g][:, :MLA_V] / accs[g][:, MLA_V:]).astype(o_ref.dtype)


def mla_attention(q, kv, kr, *, batch, seq_pad, tq):
    M = q.shape[0]
    nq = seq_pad // tq
    G = ATTN_HEADS_PER_STEP
    ng = MLA_HEADS // G
    assert tq + 128 <= seq_pad and tq > CHUNK
    return pl.pallas_call(
        functools.partial(_attn_kernel, tq=tq, seq_pad=seq_pad),
        grid=(batch, ng, nq),
        in_specs=[pl.BlockSpec((tq, G * MLA_QPAD), lambda b, h, i: (b * nq + i, h)),
                  pl.BlockSpec((seq_pad, G * 128), lambda b, h, i: (b, h)),
                  pl.BlockSpec((seq_pad, 128), lambda b, h, i: (b, 0)),
                  pl.BlockSpec((seq_pad, G * 128), lambda b, h, i: (b, ng + h))],
        out_specs=pl.BlockSpec((tq, G * MLA_V), lambda b, h, i: (b * nq + i, h)),
        out_shape=jax.ShapeDtypeStruct((M, MLA_WIDTH), BF16),
        compiler_params=_cparams("parallel", "parallel", "parallel"),
        name="mla_attention",
    )(q, kv, kr, kv)


def _s5_kernel(u_ref, br_ref, bi_ref, cr_ref, ci_ref, d_ref, ast_ref, pw_ref, o_ref,
               xr_s, xi_s, car_s, *, tt):
    n = S5_GPB * S5_STATE

    @pl.when(pl.program_id(2) == 0)
    def _():
        car_s[...] = jnp.zeros_like(car_s)

    u = u_ref[...]
    ub = u.astype(BF16)
    xr = jnp.dot(ub, br_ref[...], preferred_element_type=F32).reshape(tt // 8, 8, n)
    xi = jnp.dot(ub, bi_ref[...], preferred_element_type=F32).reshape(tt // 8, 8, n)
    for si in range(3):
        ar = ast_ref[2 * si]
        ai = ast_ref[2 * si + 1]
        sr = pltpu.roll(xr, 1 << si, axis=1)
        sm = pltpu.roll(xi, 1 << si, axis=1)
        xr, xi = xr + (ar * sr - ai * sm), xi + (ar * sm + ai * sr)
    xr_s[...] = xr.reshape(tt, n)
    xi_s[...] = xi.reshape(tt, n)
    pr = pw_ref[0]
    pi = pw_ref[1]

    def body(i, carry):
        cr, ci = carry
        sl = pl.ds(pl.multiple_of(i * 8, 8), 8)
        a = xr_s[sl, :] + (pr * cr - pi * ci)
        b = xi_s[sl, :] + (pr * ci + pi * cr)
        xr_s[sl, :] = a
        xi_s[sl, :] = b
        return (jnp.broadcast_to(a[7:8, :], (8, n)), jnp.broadcast_to(b[7:8, :], (8, n)))

    cr, ci = lax.fori_loop(0, tt // 8, body, (car_s[0], car_s[1]))
    car_s[0] = cr
    car_s[1] = ci
    y = (jnp.dot(xr_s[...].astype(BF16), cr_ref[...], preferred_element_type=F32)
         - jnp.dot(xi_s[...].astype(BF16), ci_ref[...], preferred_element_type=F32)
         + d_ref[...] * u)
    o_ref[...] = jax.nn.gelu(y).astype(o_ref.dtype)


def s5_scan(p, s5c, *, batch, seq_pad, tt):
    M = p.shape[0]
    nt = seq_pad // tt
    cw = S5_GPB * S5_GROUP
    n = S5_GPB * S5_STATE
    ublk = P_S5 // cw
    return pl.pallas_call(
        functools.partial(_s5_kernel, tt=tt),
        grid=(batch, S5_GB, nt),
        in_specs=[pl.BlockSpec((tt, cw), lambda b, g, t: (b * nt + t, ublk + g)),
                  pl.BlockSpec((None, cw, n), lambda b, g, t: (g, 0, 0)),
                  pl.BlockSpec((None, cw, n), lambda b, g, t: (g, 0, 0)),
                  pl.BlockSpec((None, n, cw), lambda b, g, t: (g, 0, 0)),
                  pl.BlockSpec((None, n, cw), lambda b, g, t: (g, 0, 0)),
                  pl.BlockSpec((None, 1, cw), lambda b, g, t: (g, 0, 0)),
                  pl.BlockSpec((None, 6, 8, n), lambda b, g, t: (g, 0, 0, 0)),
                  pl.BlockSpec((None, 2, 8, n), lambda b, g, t: (g, 0, 0, 0))],
        out_specs=pl.BlockSpec((tt, cw), lambda b, g, t: (b * nt + t, g)),
        out_shape=jax.ShapeDtypeStruct((M, S5_WIDTH), BF16),
        scratch_shapes=[pltpu.VMEM((tt, n), F32), pltpu.VMEM((tt, n), F32),
                        pltpu.VMEM((2, 8, n), F32)],
        compiler_params=_cparams("parallel", "parallel", "arbitrary"),
        name="s5_scan",
    )(p, s5c["br"], s5c["bi"], s5c["cr"], s5c["ci"], s5c["d"], s5c["ast"], s5c["pw"])


def s5_constants(a_re, a_im, log_dt, b_re, b_im, c_re, c_im, d):
    ar, ai = a_re.astype(F32), a_im.astype(F32)
    delta = jnp.exp(log_dt.astype(F32))[:, None]
    mag = jnp.exp(ar * delta)
    abar_r, abar_i = mag * jnp.cos(ai * delta), mag * jnp.sin(ai * delta)
    den = ar * ar + ai * ai
    zr = ((abar_r - 1.0) * ar + abar_i * ai) / den
    zi = (abar_i * ar - (abar_r - 1.0) * ai) / den
    br, bi = b_re.astype(F32), b_im.astype(F32)
    bbar_r = zr[..., None] * br - zi[..., None] * bi
    bbar_i = zr[..., None] * bi + zi[..., None] * br

    eye = jnp.eye(S5_GPB, dtype=F32)

    def in_blockdiag(m):
        m = m.reshape(S5_GB, S5_GPB, S5_STATE, S5_GROUP)
        bd = jnp.einsum("bgpc,gh->bgchp", m, eye)
        return bd.reshape(S5_GB, S5_GPB * S5_GROUP, S5_GPB * S5_STATE).astype(BF16)

    def out_blockdiag(m):
        m = m.reshape(S5_GB, S5_GPB, S5_GROUP, S5_STATE)
        bd = jnp.einsum("bgcp,gh->bgphc", m, eye)
        return bd.reshape(S5_GB, S5_GPB * S5_STATE, S5_GPB * S5_GROUP).astype(BF16)

    def cmul(x, y):
        return x[0] * y[0] - x[1] * y[1], x[0] * y[1] + x[1] * y[0]

    n = S5_GPB * S5_STATE
    a1 = (abar_r.reshape(S5_GB, n), abar_i.reshape(S5_GB, n))
    a2 = cmul(a1, a1)
    a4 = cmul(a2, a2)
    row = jnp.arange(8)[None, :, None]
    ast = []
    for s, a in ((1, a1), (2, a2), (4, a4)):
        for part in a:
            ast.append(jnp.where(row >= s, part[:, None, :], 0.0))
    ast = jnp.stack(ast, axis=1)
    pows = [a1]
    for _ in range(7):
        pows.append(cmul(pows[-1], a1))
    pw = jnp.stack([jnp.stack([p[0] for p in pows], axis=1),
                    jnp.stack([p[1] for p in pows], axis=1)], axis=1)
    return {
        "br": in_blockdiag(bbar_r), "bi": in_blockdiag(bbar_i),
        "cr": out_blockdiag(c_re.astype(F32)), "ci": out_blockdiag(c_im.astype(F32)),
        "d": d.astype(F32).reshape(S5_GB, 1, S5_GPB * S5_GROUP),
        "ast": ast, "pw": pw,
    }


def _dnconv_kernel(x_ref, w_ref, o_ref, buf, *, tt):
    @pl.when(pl.program_id(2) == 0)
    def _():
        buf[0:8, :] = jnp.zeros((8, buf.shape[1]), F32)

    buf[8:8 + tt, :] = x_ref[...]
    acc = buf[8:8 + tt, :] * w_ref[3:4, :]
    for j in range(DN_CONV - 1):
        lo = 8 - (DN_CONV - 1) + j
        acc = acc + buf[lo:lo + tt, :] * w_ref[j:j + 1, :]
    o_ref[...] = jax.nn.silu(acc)
    buf[0:8, :] = buf[tt:tt + 8, :]


def dn_conv(p, conv_w, *, batch, seq_pad, tt, tc):
    M = p.shape[0]
    nt = seq_pad // tt
    width = conv_w.shape[1]
    cblk = P_DNQKV // tc
    return pl.pallas_call(
        functools.partial(_dnconv_kernel, tt=tt),
        grid=(batch, width // tc, nt),
        in_specs=[pl.BlockSpec((tt, tc), lambda b, c, t: (b * nt + t, cblk + c)),
                  pl.BlockSpec((DN_CONV, tc), lambda b, c, t: (0, c))],
        out_specs=pl.BlockSpec((tt, tc), lambda b, c, t: (b * nt + t, c)),
        out_shape=jax.ShapeDtypeStruct((M, width), F32),
        scratch_shapes=[pltpu.VMEM((tt + 8, tc), F32)],
        compiler_params=_cparams("parallel", "parallel", "arbitrary"),
        name="dn_conv_silu",
    )(p, conv_w.astype(F32))


def _dn_gates_kernel(ab_ref, alog_ref, dtb_ref, gcum_ref, beta_ref):
    ab = ab_ref[...]
    g = -jnp.exp(alog_ref[...]) * jax.nn.softplus(ab[:DN_HEADS] + dtb_ref[...])
    beta_ref[...] = jax.nn.sigmoid(ab[DN_HEADS:])
    lane = lax.broadcasted_iota(jnp.int32, g.shape, 1) % CHUNK
    s = 1
    while s < CHUNK:
        g = g + jnp.where(lane >= s, pltpu.roll(g, s, axis=1), 0.0)
        s *= 2
    gcum_ref[...] = g


def dn_gates(ab_t, a_log, dt_bias):
    B, _, Lp = ab_t.shape
    col = pl.BlockSpec((DN_HEADS, 1), lambda b: (0, 0))
    out = pl.BlockSpec((None, DN_HEADS, Lp), lambda b: (b, 0, 0))
    return pl.pallas_call(
        _dn_gates_kernel,
        grid=(B,),
        in_specs=[pl.BlockSpec((None, 2 * DN_HEADS, Lp), lambda b: (b, 0, 0)), col, col],
        out_specs=[out, out],
        out_shape=[jax.ShapeDtypeStruct((B, DN_HEADS, Lp), F32)] * 2,
        compiler_params=_cparams("parallel"),
        name="dn_gates",
    )(ab_t, a_log.astype(F32).reshape(DN_HEADS, 1), dt_bias.astype(F32).reshape(DN_HEADS, 1))


def _dot(a, b):
    return jnp.dot(a, b, preferred_element_type=F32)


def _dot_nt(a, b):
    return lax.dot_general(a, b, (((1,), (1,)), ((), ())), preferred_element_type=F32)


def _dot_tn(a, b):
    return lax.dot_general(a, b, (((0,), (0,)), ((), ())), preferred_element_type=F32)


def _l2(x):
    return x * lax.rsqrt(jnp.sum(x * x, axis=-1, keepdims=True) + EPS)


def _dn_prep_kernel(q_ref, k_ref, v_ref, bcol_ref, gcol_ref, grow_ref,
                    u_ref, w_ref, qg_ref, kd_ref, qk_ref, dl_ref, *, n_chunks, unroll):
    C = CHUNK
    ii = lax.broadcasted_iota(jnp.int32, (C, C), 0)
    jj = lax.broadcasted_iota(jnp.int32, (C, C), 1)
    tri = ii >= jj
    strict = ii > jj
    eye = (ii == jj).astype(F32)

    def body(i, carry):
        U = range(unroll)
        cs = [i * unroll + j for j in U]
        sls = [pl.ds(pl.multiple_of(c * C, C), C) for c in cs]
        q = [_l2(q_ref[sl, :]) * (DN_DK ** -0.5) for sl in sls]
        k = [_l2(k_ref[sl, :]) for sl in sls]
        beta = [bcol_ref[sl, :] for sl in sls]
        gc = [gcol_ref[sl, :] for sl in sls]
        decay = [jnp.exp(jnp.where(tri, gc[j] - grow_ref[pl.ds(cs[j], 1), :], -jnp.inf)) for j in U]
        kb = [k[j] * beta[j] for j in U]
        a_mat = [jnp.where(strict, _dot_nt(kb[j], k[j]) * decay[j], 0.0) for j in U]
        t_inv = [eye - a for a in a_mat]
        pw = a_mat
        for _ in range(5):
            pw = [_dot(x, x) for x in pw]
            t_inv = [t + _dot(t, x) for t, x in zip(t_inv, pw)]
        u = [_dot(t_inv[j], v_ref[sls[j], :] * beta[j]) for j in U]
        w = [_dot(t_inv[j], kb[j] * jnp.exp(gc[j])) for j in U]
        qk = [_dot_nt(q[j].astype(BF16), k[j].astype(BF16)) * decay[j] for j in U]
        for j in U:
            sl = sls[j]
            u_ref[sl, :] = u[j]
            w_ref[sl, :] = w[j].astype(w_ref.dtype)
            qk_ref[sl, :] = qk[j].astype(qk_ref.dtype)
            qg_ref[sl, :] = (q[j] * jnp.exp(gc[j])).astype(qg_ref.dtype)
            g_last = gc[j][C - 1:C, :]
            kd_ref[sl, :] = (k[j] * jnp.exp(g_last - gc[j])).astype(kd_ref.dtype)
            dl_ref[pl.ds(cs[j], 1), :] = jnp.broadcast_to(jnp.exp(g_last), (1, dl_ref.shape[1]))
        return carry

    lax.fori_loop(0, n_chunks // unroll, body, 0)


def dn_prep(qkv, beta_col, gcum_col, gcum_row, *, batch, seq_pad):
    M = qkv.shape[0]
    H = DN_HEADS
    nc = seq_pad // CHUNK

    def head(off):
        return pl.BlockSpec((seq_pad, 128), lambda b, h: (b, off + h))

    colspec = pl.BlockSpec((None, None, seq_pad, 1), lambda b, h: (b, h, 0, 0))
    wide = jax.ShapeDtypeStruct((M, DN_WIDTH), BF16)
    return pl.pallas_call(
        functools.partial(_dn_prep_kernel, n_chunks=nc, unroll=_pick(nc, (6, 4, 3, 2, 1))),
        grid=(batch, H),
        in_specs=[head(0), head(H), head(2 * H), colspec, colspec,
                  pl.BlockSpec((None, None, nc, CHUNK), lambda b, h: (b, h, 0, 0))],
        out_specs=[head(0), head(0), head(0), head(0),
                   pl.BlockSpec((None, None, seq_pad, CHUNK), lambda b, h: (b, h, 0, 0)),
                   pl.BlockSpec((None, None, nc, 128), lambda b, h: (b, h, 0, 0))],
        out_shape=[jax.ShapeDtypeStruct((M, DN_WIDTH), F32), wide, wide, wide,
                   jax.ShapeDtypeStruct((batch, H, seq_pad, CHUNK), BF16),
                   jax.ShapeDtypeStruct((batch, H, nc, 128), F32)],
        compiler_params=_cparams("parallel", "parallel"),
        name="dn_prep",
    )(qkv, qkv, qkv, beta_col, gcum_col, gcum_row)


def _dn_scan_kernel(u_ref, w_ref, qg_ref, kd_ref, qk_ref, dl_ref, z_ref, gn_ref, o_ref, s_ref,
                    *, chunks_per_tile):
    C = CHUNK

    @pl.when(pl.program_id(1) == 0)
    def _():
        s_ref[...] = jnp.zeros_like(s_ref)

    gn = gn_ref[...]

    def body(c, carry):
        sl = pl.ds(pl.multiple_of(c * C, C), C)
        H = range(DN_HEADS)
        hs = [slice(j * DN_DV, (j + 1) * DN_DV) for j in H]
        S = [s_ref[j] for j in H]
        Sb = [s.astype(BF16) for s in S]
        v_new = [u_ref[sl, hs[j]] - _dot(w_ref[sl, hs[j]], Sb[j]) for j in H]
        o_inter = [_dot(qg_ref[sl, hs[j]], Sb[j]) for j in H]
        vb = [v.astype(BF16) for v in v_new]
        o = [o_inter[j] + _dot(qk_ref[j, sl, :], vb[j]) for j in H]
        s_upd = [_dot_tn(kd_ref[sl, hs[j]], vb[j]) for j in H]
        for j in H:
            s_ref[j] = S[j] * dl_ref[j, pl.ds(c, 1), :] + s_upd[j]
            o_ref[sl, hs[j]] = (_rms(o[j], gn) * jax.nn.silu(z_ref[sl, hs[j]])).astype(o_ref.dtype)
        return carry

    lax.fori_loop(0, chunks_per_tile, body, 0)


def dn_scan(u, w, qg, kd, qk, dl, p, out_norm_g, *, batch, seq_pad):
    M = p.shape[0]
    H = DN_HEADS
    nc = seq_pad // CHUNK
    cpt = _pick(nc, (11, 6, 4, 3, 2, 1))
    nt = nc // cpt
    tt = cpt * CHUNK
    dl = dl.reshape(batch, H, nt, cpt, 128)

    def rows(col_block=0):
        return pl.BlockSpec((tt, DN_WIDTH), lambda b, t: (b * nt + t, col_block))

    return pl.pallas_call(
        functools.partial(_dn_scan_kernel, chunks_per_tile=cpt),
        grid=(batch, nt),
        in_specs=[rows(), rows(), rows(), rows(),
                  pl.BlockSpec((None, H, tt, CHUNK), lambda b, t: (b, 0, t, 0)),
                  pl.BlockSpec((None, H, None, cpt, 128), lambda b, t: (b, 0, t, 0, 0)),
                  rows(P_DNZ // DN_WIDTH),
                  pl.BlockSpec((1, DN_DV), lambda b, t: (0, 0))],
        out_specs=rows(),
        out_shape=jax.ShapeDtypeStruct((M, DN_WIDTH), BF16),
        scratch_shapes=[pltpu.VMEM((H, DN_DK, DN_DV), F32)],
        compiler_params=_cparams("parallel", "arbitrary"),
        name="dn_scan",
    )(u, w, qg, kd, qk, dl, p, out_norm_g.astype(F32).reshape(1, DN_DV))


def _rot_cols(w):
    half = w.shape[-1] // 2
    return jnp.concatenate([-w[..., half:], w[..., :half]], axis=-1)


def _prep_mixer(w_in, mla_w_uq, mla_w_ukv):
    depth = w_in.shape[0]
    kr = w_in[..., OFF_KR:OFF_S5]
    w_small = jnp.concatenate([
        w_in[..., OFF_Q:OFF_KV], w_in[..., OFF_DN_Z:OFF_DN_A], w_in[..., OFF_S5:OFF_DN_Z],
        w_in[..., OFF_KV:OFF_KR], kr, _rot_cols(kr), w_in[..., OFF_DN_A:OFF_GATE],
        jnp.zeros((depth, D_MODEL, 128 - 2 * DN_HEADS), w_in.dtype)], axis=-1).astype(BF16)
    w_gate = w_in[..., OFF_GATE:].astype(BF16)
    uq = mla_w_uq.reshape(depth, MLA_Q_LORA, MLA_HEADS, MLA_QK)
    w_uq = jnp.concatenate([uq, _rot_cols(uq[..., MLA_NOPE:])], axis=-1)
    w_uq = w_uq.reshape(depth, MLA_Q_LORA, MLA_HEADS * MLA_QPAD).astype(BF16)
    ukv = mla_w_ukv.reshape(depth, MLA_KV_LORA, MLA_HEADS, 2, 128)
    w_ukv = jnp.transpose(ukv, (0, 1, 3, 2, 4)).reshape(depth, MLA_KV_LORA, 2 * MLA_HEADS * 128)
    return w_small, w_gate, w_uq, w_ukv.astype(BF16)


def _prep_ffn_down(w2):
    return jnp.pad(w2.astype(BF16), ((0, 0), (0, FFN_PAD - FFN_HIDDEN), (0, 0)))


def _rope_tables(seq_pad, batch):
    inv = ROPE_THETA ** (-jnp.arange(0, MLA_ROPE, 2, dtype=F32) / MLA_ROPE)
    ang = jnp.arange(seq_pad, dtype=F32)[:, None] * inv[None, :]
    zeros = jnp.zeros((seq_pad, 128 - MLA_ROPE), F32)
    cos_t = jnp.concatenate([jnp.cos(ang), jnp.cos(ang), zeros], axis=1)
    sin_t = jnp.concatenate([jnp.sin(ang), jnp.sin(ang), zeros], axis=1)
    return jnp.tile(cos_t, (batch, 1)), jnp.tile(sin_t, (batch, 1))


def _ffn(xn, w13, w2, layer, *, tm):
    hid = matmul_swiglu(xn, w13, layer, tm=tm, tn=256, n_out=FFN_PAD)
    return matmul(hid, w2, layer, tm=tm, tn=1024, tk=FFN_PAD // 4, out_dtype=F32, name="ffn_down")


def kernel(x, meta_tokens, sandwich_g, ffn1_w13, ffn1_w2, w_in, mla_q_norm_g, mla_kv_norm_g, mla_w_uq, mla_w_ukv, mla_w_o, s5_a_re, s5_a_im, s5_log_dt, s5_b_re, s5_b_im, s5_c_re, s5_c_im, s5_d, s5_w_glu, dn_conv_w, dn_a_log, dn_dt_bias, dn_out_norm_g, dn_w_o, w_out, ffn2_w13, ffn2_w2):
    B, seq, D = x.shape
    depth = w_in.shape[0]
    L = N_META + seq
    assert (L - N_META) % CHUNK == 0, "pad keys are hidden by the chunk mask only on a chunk boundary"
    Lp = -(-L // 128) * 128
    M = B * Lp
    tm = _pick(M, (1056, 768, 384, 128))
    tr = _pick(M, (176, 128))
    tt = _pick(Lp, (528, 384, 128))
    ta = _pick(Lp, tuple(c for c in (384, 128) if c + 128 <= Lp))

    meta = jnp.broadcast_to(meta_tokens[None].astype(x.dtype), (B, N_META, D))
    h = jnp.concatenate([meta, x, jnp.zeros((B, Lp - L, D), x.dtype)], axis=1).reshape(M, D)
    cos_t, sin_t = _rope_tables(Lp, B)

    f1_w13, f1_w2 = ffn1_w13, _prep_ffn_down(ffn1_w2)
    f2_w13, f2_w2 = ffn2_w13, _prep_ffn_down(ffn2_w2)
    w_small, w_gate, w_uq, w_ukv = _prep_mixer(w_in, mla_w_uq, mla_w_ukv)
    w_mla_o, w_glu, w_dn_o, w_mix = (t.astype(BF16) for t in (mla_w_o, s5_w_glu, dn_w_o, w_out))
    q_scale = MLA_QK ** -0.5 * math.log2(math.e)

    xn = rmsnorm_cast(h, sandwich_g[0, 0], col_block=0, width=D, tm=tr)
    for l in range(depth):
        g = sandwich_g[l]
        y = _ffn(xn, f1_w13, f1_w2, l, tm=tm)
        h, xn = resid_norm(h, y, g[1], g[2], coef=0.5, tm=tr)

        p = matmul(xn, w_small, l, tm=tm, tn=768, tk=D, out_dtype=F32, name="in_proj_small")
        gates = matmul(xn, w_gate, l, tm=tm, tn=1024, tk=D, out_dtype=BF16, act="sigmoid",
                       name="in_proj_gates")

        qn = rmsnorm_cast(p, mla_q_norm_g[l], col_block=P_Q // MLA_Q_LORA, width=MLA_Q_LORA, tm=tm)
        kvn = rmsnorm_cast(p, mla_kv_norm_g[l], col_block=P_KV // MLA_KV_LORA, width=MLA_KV_LORA, tm=tm)
        q = matmul_qrope(qn, w_uq, l, cos_t, sin_t, tm=tm, heads_per_tile=4, scale=q_scale)
        kv = matmul(kvn, w_ukv, l, tm=tm, tn=1024, tk=MLA_KV_LORA, out_dtype=BF16, name="mla_kv_proj")
        kr = k_rope(p, cos_t, sin_t, tm=tm)
        o_mla = mla_attention(q, kv, kr, batch=B, seq_pad=Lp, tq=ta)

        s5c = s5_constants(s5_a_re[l], s5_a_im[l], s5_log_dt[l], s5_b_re[l], s5_b_im[l],
                           s5_c_re[l], s5_c_im[l], s5_d[l])
        h_s5 = s5_scan(p, s5c, batch=B, seq_pad=Lp, tt=tt)

        qkv = dn_conv(p, dn_conv_w[l], batch=B, seq_pad=Lp, tt=tt, tc=512)
        ab_t = jnp.transpose(p[:, P_AB:P_AB + 2 * DN_HEADS].reshape(B, Lp, 2 * DN_HEADS), (0, 2, 1))
        gcum, beta = dn_gates(ab_t, dn_a_log[l], dn_dt_bias[l])
        u, w, qg, kd, qk, dl = dn_prep(qkv, beta[..., None], gcum[..., None],
                                       gcum.reshape(B, DN_HEADS, Lp // CHUNK, CHUNK),
                                       batch=B, seq_pad=Lp)
        o_dn = dn_scan(u, w, qg, kd, qk, dl, p, dn_out_norm_g[l], batch=B, seq_pad=Lp)

        merged = merge_branches(o_mla, h_s5, o_dn, w_mla_o, w_glu, w_dn_o, gates, l, tm=tm, tn=512)
        mix = matmul(merged, w_mix, l, tm=tm, tn=1024, tk=D, out_dtype=F32, name="out_proj")
        h, xn = resid_norm(h, mix, g[3], g[4], coef=1.0, tm=tr)

        y = _ffn(xn, f2_w13, f2_w2, l, tm=tm)
        g_next = sandwich_g[l + 1, 0] if l + 1 < depth else None
        h, xn = resid_norm(h, y, g[5], g_next, coef=0.5, tm=tr)
    return h.reshape(B, Lp, D)[:, N_META:L]
```

```python
import functools
import math

import jax
import jax.numpy as jnp
from jax import lax
from jax.experimental import pallas as pl
from jax.experimental.pallas import tpu as pltpu

F32 = jnp.float32
BF16 = jnp.bfloat16

D_MODEL = 4096
CHUNK = 64
N_META = 16
EPS = 1e-6

MLA_HEADS = 16
MLA_Q_LORA = 1024
MLA_KV_LORA = 512
MLA_NOPE = 128
MLA_ROPE = 64
MLA_V = 128
MLA_QK = MLA_NOPE + MLA_ROPE
MLA_WIDTH = MLA_HEADS * MLA_V
MLA_QPAD = 256
ROPE_THETA = 10000.0

S5_WIDTH = 1024
S5_GROUP = 16
S5_GROUPS = S5_WIDTH // S5_GROUP
S5_STATE = 64
S5_GB = 4
S5_GPB = S5_GROUPS // S5_GB

DN_HEADS = 8
DN_DK = 128
DN_DV = 128
DN_QK = DN_HEADS * DN_DK
DN_WIDTH = DN_HEADS * DN_DV
DN_CONV = 4

FFN_HIDDEN = 11008

OFF_Q = 0
OFF_KV = OFF_Q + MLA_Q_LORA
OFF_KR = OFF_KV + MLA_KV_LORA
OFF_S5 = OFF_KR + MLA_ROPE
OFF_DN_QKV = OFF_S5 + S5_WIDTH
OFF_DN_Z = OFF_DN_QKV + 2 * DN_QK + DN_WIDTH
OFF_DN_A = OFF_DN_Z + DN_WIDTH
OFF_DN_B = OFF_DN_A + DN_HEADS
OFF_GATE = OFF_DN_B + DN_HEADS

P_Q = 0
P_DNZ = P_Q + MLA_Q_LORA
P_S5 = P_DNZ + DN_WIDTH
P_DNQKV = P_S5 + S5_WIDTH
P_KV = P_DNQKV + 3 * DN_QK
P_KR = P_KV + MLA_KV_LORA
P_AB = P_KR + 128
P_WIDTH = P_AB + 128

VMEM_LIMIT_BYTES = 56 * 1024 * 1024


def _cparams(*sem):
    return pltpu.CompilerParams(dimension_semantics=sem, vmem_limit_bytes=VMEM_LIMIT_BYTES)


def _pick(n, candidates):
    for c in candidates:
        if n % c == 0:
            return c
    raise ValueError(f"no tile in {candidates} divides {n}")


def _rms(x, g):
    return x * lax.rsqrt(jnp.mean(x * x, axis=-1, keepdims=True) + EPS) * g


def _rmsnorm_kernel(x_ref, g_ref, o_ref):
    o_ref[...] = _rms(x_ref[...], g_ref[...]).astype(o_ref.dtype)


def rmsnorm_cast(x, g, *, col_block, width, tm):
    M = x.shape[0]
    return pl.pallas_call(
        _rmsnorm_kernel,
        grid=(M // tm,),
        in_specs=[pl.BlockSpec((tm, width), lambda i: (i, col_block)),
                  pl.BlockSpec((1, width), lambda i: (0, 0))],
        out_specs=pl.BlockSpec((tm, width), lambda i: (i, 0)),
        out_shape=jax.ShapeDtypeStruct((M, width), BF16),
        compiler_params=_cparams("parallel"),
        name="rmsnorm_cast",
    )(x, g.reshape(1, width).astype(F32))


def _resid_norm_kernel(h_ref, y_ref, gpost_ref, gpre_ref, ho_ref, xn_ref, *, coef):
    h = h_ref[...] + coef * _rms(y_ref[...], gpost_ref[...])
    ho_ref[...] = h
    xn_ref[...] = _rms(h, gpre_ref[...]).astype(xn_ref.dtype)


def _resid_kernel(h_ref, y_ref, gpost_ref, ho_ref, *, coef):
    ho_ref[...] = h_ref[...] + coef * _rms(y_ref[...], gpost_ref[...])


def resid_norm(h, y, g_post, g_pre, *, coef, tm):
    M, D = h.shape
    row = pl.BlockSpec((tm, D), lambda i: (i, 0))
    gain = pl.BlockSpec((1, D), lambda i: (0, 0))
    if g_pre is None:
        return pl.pallas_call(
            functools.partial(_resid_kernel, coef=coef),
            grid=(M // tm,), in_specs=[row, row, gain], out_specs=row,
            out_shape=jax.ShapeDtypeStruct((M, D), F32),
            input_output_aliases={0: 0},
            compiler_params=_cparams("parallel"), name="resid",
        )(h, y, g_post.reshape(1, D)), None
    return pl.pallas_call(
        functools.partial(_resid_norm_kernel, coef=coef),
        grid=(M // tm,), in_specs=[row, row, gain, gain], out_specs=[row, row],
        out_shape=[jax.ShapeDtypeStruct((M, D), F32), jax.ShapeDtypeStruct((M, D), BF16)],
        input_output_aliases={0: 0},
        compiler_params=_cparams("parallel"), name="resid_norm",
    )(h, y, g_post.reshape(1, D), g_pre.reshape(1, D))


def _mm_kernel(x_ref, w_ref, o_ref, *, nk, act, w_transposed):
    if w_transposed:
        part = lax.dot_general(x_ref[...], w_ref[...], (((1,), (1,)), ((), ())),
                               preferred_element_type=F32)
    else:
        part = jnp.dot(x_ref[...], w_ref[...], preferred_element_type=F32)
    if nk == 1:
        if act == "sigmoid":
            part = jax.nn.sigmoid(part)
        o_ref[...] = part.astype(o_ref.dtype)
        return

    @pl.when(pl.program_id(2) == 0)
    def _():
        o_ref[...] = jnp.zeros_like(o_ref)

    o_ref[...] += part


def matmul(x, w, layer, *, tm, tn, tk, out_dtype, act=None, w_transposed=False, name="matmul"):
    M, K = x.shape
    N = w.shape[1] if w_transposed else w.shape[2]
    nk = K // tk
    assert nk == 1 or (out_dtype == F32 and act is None), "K-split accumulates in an f32 output"
    if w_transposed:
        w_spec = pl.BlockSpec((None, tn, tk), lambda i, j, k: (layer, j, k))
    else:
        w_spec = pl.BlockSpec((None, tk, tn), lambda i, j, k: (layer, k, j))
    return pl.pallas_call(
        functools.partial(_mm_kernel, nk=nk, act=act, w_transposed=w_transposed),
        grid=(M // tm, N // tn, nk),
        in_specs=[pl.BlockSpec((tm, tk), lambda i, j, k: (i, k)), w_spec],
        out_specs=pl.BlockSpec((tm, tn), lambda i, j, k: (i, j)),
        out_shape=jax.ShapeDtypeStruct((M, N), out_dtype),
        compiler_params=_cparams("parallel", "parallel", "arbitrary"),
        name=name,
    )(x, w)


def _mm_swiglu_kernel(x_ref, wa_ref, wb_ref, o_ref):
    x = x_ref[...]
    a = jnp.dot(x, wa_ref[...].astype(BF16), preferred_element_type=F32)
    b = jnp.dot(x, wb_ref[...].astype(BF16), preferred_element_type=F32)
    o_ref[...] = (jax.nn.silu(a) * b).astype(o_ref.dtype)


def matmul_swiglu(x, w13, layer, *, tm, tn):
    M, K = x.shape
    N = w13.shape[2] // 2
    nj = N // tn
    return pl.pallas_call(
        _mm_swiglu_kernel,
        grid=(M // tm, nj),
        in_specs=[pl.BlockSpec((tm, K), lambda i, j: (i, 0)),
                  pl.BlockSpec((None, K, tn), lambda i, j: (layer, 0, j)),
                  pl.BlockSpec((None, K, tn), lambda i, j: (layer, 0, nj + j))],
        out_specs=pl.BlockSpec((tm, tn), lambda i, j: (i, j)),
        out_shape=jax.ShapeDtypeStruct((M, N), BF16),
        compiler_params=_cparams("parallel", "parallel"),
        name="ffn_up_swiglu",
    )(x, w13, w13)


def _rope_block(a, c, s):
    return a * c + pltpu.roll(a, 64, axis=1) * s


def _mm_qrope_kernel(x_ref, w_ref, c_ref, s_ref, o_ref, *, heads, scale):
    acc = jnp.dot(x_ref[...], w_ref[...], preferred_element_type=F32) * scale
    c = c_ref[...]
    s = s_ref[...]
    for hb in range(heads):
        lo = hb * MLA_QPAD
        o_ref[:, lo:lo + 128] = acc[:, lo:lo + 128].astype(o_ref.dtype)
        o_ref[:, lo + 128:lo + 256] = _rope_block(acc[:, lo + 128:lo + 256], c, s).astype(o_ref.dtype)


def matmul_qrope(x, w, layer, cos_t, sin_t, *, tm, heads_per_tile, scale):
    M, K = x.shape
    N = w.shape[2]
    tn = heads_per_tile * MLA_QPAD
    tab = pl.BlockSpec((tm, 128), lambda i, j: (i, 0))
    return pl.pallas_call(
        functools.partial(_mm_qrope_kernel, heads=heads_per_tile, scale=scale),
        grid=(M // tm, N // tn),
        in_specs=[pl.BlockSpec((tm, K), lambda i, j: (i, 0)),
                  pl.BlockSpec((None, K, tn), lambda i, j: (layer, 0, j)), tab, tab],
        out_specs=pl.BlockSpec((tm, tn), lambda i, j: (i, j)),
        out_shape=jax.ShapeDtypeStruct((M, N), BF16),
        compiler_params=_cparams("parallel", "parallel"),
        name="mla_q_proj_rope",
    )(x, w, cos_t, sin_t)


def _krope_kernel(x_ref, c_ref, s_ref, o_ref):
    o_ref[...] = _rope_block(x_ref[...], c_ref[...], s_ref[...]).astype(o_ref.dtype)


def k_rope(p, cos_t, sin_t, *, tm):
    M = p.shape[0]
    tab = pl.BlockSpec((tm, 128), lambda i: (i, 0))
    return pl.pallas_call(
        _krope_kernel,
        grid=(M // tm,),
        in_specs=[pl.BlockSpec((tm, 128), lambda i: (i, P_KR // 128)), tab, tab],
        out_specs=tab,
        out_shape=jax.ShapeDtypeStruct((M, 128), BF16),
        compiler_params=_cparams("parallel"),
        name="mla_k_rope",
    )(p, cos_t, sin_t)


def _merge_kernel(o_ref, hs_ref, dn_ref, wo_ref, wv_ref, wg_ref, wd_ref,
                  g0_ref, g1_ref, g2_ref, out_ref):
    hs = hs_ref[...]
    y_mla = jnp.dot(o_ref[...], wo_ref[...], preferred_element_type=F32)
    val = jnp.dot(hs, wv_ref[...], preferred_element_type=F32)
    gate = jnp.dot(hs, wg_ref[...], preferred_element_type=F32)
    y_dn = jnp.dot(dn_ref[...], wd_ref[...], preferred_element_type=F32)
    merged = (g0_ref[...].astype(F32) * y_mla
              + g1_ref[...].astype(F32) * (val * jax.nn.sigmoid(gate))
              + g2_ref[...].astype(F32) * y_dn)
    out_ref[...] = merged.astype(out_ref.dtype)


def merge_branches(o_mla, h_s5, o_dn, w_o, w_glu, w_dn_o, gates, layer, *, tm, tn):
    M = o_mla.shape[0]
    D = w_o.shape[2]
    nj = D // tn

    def rows(width):
        return pl.BlockSpec((tm, width), lambda i, j: (i, 0))

    def cols(kdim, off):
        return pl.BlockSpec((None, kdim, tn), lambda i, j: (layer, 0, off * nj + j))

    def gate(b):
        return pl.BlockSpec((tm, tn), lambda i, j: (i, b * nj + j))

    return pl.pallas_call(
        _merge_kernel,
        grid=(M // tm, nj),
        in_specs=[rows(MLA_WIDTH), rows(S5_WIDTH), rows(DN_WIDTH),
                  cols(MLA_WIDTH, 0), cols(S5_WIDTH, 0), cols(S5_WIDTH, 1), cols(DN_WIDTH, 0),
                  gate(0), gate(1), gate(2)],
        out_specs=pl.BlockSpec((tm, tn), lambda i, j: (i, j)),
        out_shape=jax.ShapeDtypeStruct((M, D), BF16),
        compiler_params=_cparams("parallel", "parallel"),
        name="merge_branches",
    )(o_mla, h_s5, o_dn, w_o, w_glu, w_glu, w_dn_o, gates, gates, gates)


_NEG = -1e30


def _chunk_id(pos):
    return (pos + (CHUNK - N_META)) >> 6


ATTN_HEADS_PER_STEP = 4


def _attn_kernel(q_ref, kn_ref, kr_ref, v_ref, o_ref, *, tq, seq_pad):
    G = ATTN_HEADS_PER_STEP
    tk = tq
    tkm = tq + 128
    q0 = pl.program_id(2) * tq
    qs = [q_ref[:, g * MLA_QPAD:(g + 1) * MLA_QPAD] for g in range(G)]
    full_end = (_chunk_id(q0) + 1) * CHUNK - (CHUNK - N_META)
    n_full = full_end // tk

    def step(ks, width, carry, mask):
        ms, accs = carry
        k_rope = kr_ref[pl.ds(ks, width), :]
        ones = jnp.ones((width, 128), BF16)

        def scores(g):
            k = jnp.concatenate([kn_ref[pl.ds(ks, width), g * 128:(g + 1) * 128], k_rope], axis=1)
            s = lax.dot_general(qs[g], k, (((1,), (1,)), ((), ())), preferred_element_type=F32)
            return s if mask is None else jnp.where(mask, s, _NEG)

        def weights(g, s):
            m_new = jnp.maximum(ms[g], jnp.max(s, axis=-1, keepdims=True))
            return m_new, jnp.exp2(s - m_new).astype(BF16), jnp.exp2(ms[g] - m_new)

        def update(g, p, alpha):
            v1 = jnp.concatenate([v_ref[pl.ds(ks, width), g * 128:(g + 1) * 128], ones], axis=1)
            return alpha * accs[g] + jnp.dot(p, v1, preferred_element_type=F32)

        ss = [scores(g) for g in range(G)]
        m_new, ps, alphas = zip(*[weights(g, ss[g]) for g in range(G)])
        return list(m_new), [update(g, ps[g], alphas[g]) for g in range(G)]

    def full_step(j, carry):
        return step(pl.multiple_of(j * tk, tk), tk, carry, None)

    init = ([jnp.full((tq, 1), _NEG, F32)] * G, [jnp.zeros((tq, 2 * MLA_V), F32)] * G)
    carry = lax.fori_loop(0, n_full, full_step, init)

    first = n_full * tk
    ks = pl.multiple_of(jnp.minimum(first, seq_pad - tkm), 128)
    kpos = ks + lax.broadcasted_iota(jnp.int32, (1, tkm), 1)
    cq = _chunk_id(q0 + lax.broadcasted_iota(jnp.int32, (tq, 1), 0))
    mask = jnp.logical_and(cq >= _chunk_id(kpos), kpos >= first)
    _, accs = step(ks, tkm, carry, mask)
    for g in range(G):
        o_ref[:, g * MLA_V:(g + 1) * MLA_V] = (
            accs[g][:, :MLA_V] / accs[g][:, MLA_V:]).astype(o_ref.dtype)


def mla_attention(q, kv, kr, *, batch, seq_pad, tq):
    M = q.shape[0]
    nq = seq_pad // tq
    G = ATTN_HEADS_PER_STEP
    ng = MLA_HEADS // G
    assert tq + 128 <= seq_pad and tq > CHUNK
    return pl.pallas_call(
        functools.partial(_attn_kernel, tq=tq, seq_pad=seq_pad),
        grid=(batch, ng, nq),
        in_specs=[pl.BlockSpec((tq, G * MLA_QPAD), lambda b, h, i: (b * nq + i, h)),
                  pl.BlockSpec((seq_pad, G * 128), lambda b, h, i: (b, h)),
                  pl.BlockSpec((seq_pad, 128), lambda b, h, i: (b, 0)),
                  pl.BlockSpec((seq_pad, G * 128), lambda b, h, i: (b, ng + h))],
        out_specs=pl.BlockSpec((tq, G * MLA_V), lambda b, h, i: (b * nq + i, h)),
        out_shape=jax.ShapeDtypeStruct((M, MLA_WIDTH), BF16),
        compiler_params=_cparams("parallel", "parallel", "parallel"),
        name="mla_attention",
    )(q, kv, kr, kv)


def _s5_kernel(u_ref, br_ref, bi_ref, cr_ref, ci_ref, d_ref, ast_ref, pw_ref, o_ref,
               xr_s, xi_s, car_s, *, tt):
    n = S5_GPB * S5_STATE

    @pl.when(pl.program_id(2) == 0)
    def _():
        car_s[...] = jnp.zeros_like(car_s)

    u = u_ref[...]
    ub = u.astype(BF16)
    xr = jnp.dot(ub, br_ref[...], preferred_element_type=F32).reshape(tt // 8, 8, n)
    xi = jnp.dot(ub, bi_ref[...], preferred_element_type=F32).reshape(tt // 8, 8, n)
    for si in range(3):
        ar = ast_ref[2 * si]
        ai = ast_ref[2 * si + 1]
        sr = pltpu.roll(xr, 1 << si, axis=1)
        sm = pltpu.roll(xi, 1 << si, axis=1)
        xr, xi = xr + (ar * sr - ai * sm), xi + (ar * sm + ai * sr)
    xr_s[...] = xr.reshape(tt, n)
    xi_s[...] = xi.reshape(tt, n)
    pr = pw_ref[0]
    pi = pw_ref[1]

    def body(i, carry):
        cr, ci = carry
        sl = pl.ds(pl.multiple_of(i * 8, 8), 8)
        a = xr_s[sl, :] + (pr * cr - pi * ci)
        b = xi_s[sl, :] + (pr * ci + pi * cr)
        xr_s[sl, :] = a
        xi_s[sl, :] = b
        return (jnp.broadcast_to(a[7:8, :], (8, n)), jnp.broadcast_to(b[7:8, :], (8, n)))

    cr, ci = lax.fori_loop(0, tt // 8, body, (car_s[0], car_s[1]))
    car_s[0] = cr
    car_s[1] = ci
    y = (jnp.dot(xr_s[...].astype(BF16), cr_ref[...], preferred_element_type=F32)
         - jnp.dot(xi_s[...].astype(BF16), ci_ref[...], preferred_element_type=F32)
         + d_ref[...] * u)
    o_ref[...] = jax.nn.gelu(y).astype(o_ref.dtype)


def s5_scan(p, s5c, *, batch, seq_pad, tt):
    M = p.shape[0]
    nt = seq_pad // tt
    cw = S5_GPB * S5_GROUP
    n = S5_GPB * S5_STATE
    ublk = P_S5 // cw
    return pl.pallas_call(
        functools.partial(_s5_kernel, tt=tt),
        grid=(batch, S5_GB, nt),
        in_specs=[pl.BlockSpec((tt, cw), lambda b, g, t: (b * nt + t, ublk + g)),
                  pl.BlockSpec((None, cw, n), lambda b, g, t: (g, 0, 0)),
                  pl.BlockSpec((None, cw, n), lambda b, g, t: (g, 0, 0)),
                  pl.BlockSpec((None, n, cw), lambda b, g, t: (g, 0, 0)),
                  pl.BlockSpec((None, n, cw), lambda b, g, t: (g, 0, 0)),
                  pl.BlockSpec((None, 1, cw), lambda b, g, t: (g, 0, 0)),
                  pl.BlockSpec((None, 6, 8, n), lambda b, g, t: (g, 0, 0, 0)),
                  pl.BlockSpec((None, 2, 8, n), lambda b, g, t: (g, 0, 0, 0))],
        out_specs=pl.BlockSpec((tt, cw), lambda b, g, t: (b * nt + t, g)),
        out_shape=jax.ShapeDtypeStruct((M, S5_WIDTH), BF16),
        scratch_shapes=[pltpu.VMEM((tt, n), F32), pltpu.VMEM((tt, n), F32),
                        pltpu.VMEM((2, 8, n), F32)],
        compiler_params=_cparams("parallel", "parallel", "arbitrary"),
        name="s5_scan",
    )(p, s5c["br"], s5c["bi"], s5c["cr"], s5c["ci"], s5c["d"], s5c["ast"], s5c["pw"])


def s5_constants(a_re, a_im, log_dt, b_re, b_im, c_re, c_im, d):
    ar, ai = a_re.astype(F32), a_im.astype(F32)
    delta = jnp.exp(log_dt.astype(F32))[:, None]
    mag = jnp.exp(ar * delta)
    abar_r, abar_i = mag * jnp.cos(ai * delta), mag * jnp.sin(ai * delta)
    den = ar * ar + ai * ai
    zr = ((abar_r - 1.0) * ar + abar_i * ai) / den
    zi = (abar_i * ar - (abar_r - 1.0) * ai) / den
    br, bi = b_re.astype(F32), b_im.astype(F32)
    bbar_r = zr[..., None] * br - zi[..., None] * bi
    bbar_i = zr[..., None] * bi + zi[..., None] * br

    eye = jnp.eye(S5_GPB, dtype=F32)

    def in_blockdiag(m):
        m = m.reshape(S5_GB, S5_GPB, S5_STATE, S5_GROUP)
        bd = jnp.einsum("bgpc,gh->bgchp", m, eye)
        return bd.reshape(S5_GB, S5_GPB * S5_GROUP, S5_GPB * S5_STATE).astype(BF16)

    def out_blockdiag(m):
        m = m.reshape(S5_GB, S5_GPB, S5_GROUP, S5_STATE)
        bd = jnp.einsum("bgcp,gh->bgphc", m, eye)
        return bd.reshape(S5_GB, S5_GPB * S5_STATE, S5_GPB * S5_GROUP).astype(BF16)

    def cmul(x, y):
        return x[0] * y[0] - x[1] * y[1], x[0] * y[1] + x[1] * y[0]

    n = S5_GPB * S5_STATE
    a1 = (abar_r.reshape(S5_GB, n), abar_i.reshape(S5_GB, n))
    a2 = cmul(a1, a1)
    a4 = cmul(a2, a2)
    row = jnp.arange(8)[None, :, None]
    ast = []
    for s, a in ((1, a1), (2, a2), (4, a4)):
        for part in a:
            ast.append(jnp.where(row >= s, part[:, None, :], 0.0))
    ast = jnp.stack(ast, axis=1)
    pows = [a1]
    for _ in range(7):
        pows.append(cmul(pows[-1], a1))
    pw = jnp.stack([jnp.stack([p[0] for p in pows], axis=1),
                    jnp.stack([p[1] for p in pows], axis=1)], axis=1)
    return {
        "br": in_blockdiag(bbar_r), "bi": in_blockdiag(bbar_i),
        "cr": out_blockdiag(c_re.astype(F32)), "ci": out_blockdiag(c_im.astype(F32)),
        "d": d.astype(F32).reshape(S5_GB, 1, S5_GPB * S5_GROUP),
        "ast": ast, "pw": pw,
    }


def _dnconv_kernel(x_ref, w_ref, o_ref, buf, *, tt):
    @pl.when(pl.program_id(2) == 0)
    def _():
        buf[0:8, :] = jnp.zeros((8, buf.shape[1]), F32)

    buf[8:8 + tt, :] = x_ref[...]
    acc = buf[8:8 + tt, :] * w_ref[3:4, :]
    for j in range(DN_CONV - 1):
        lo = 8 - (DN_CONV - 1) + j
        acc = acc + buf[lo:lo + tt, :] * w_ref[j:j + 1, :]
    o_ref[...] = jax.nn.silu(acc)
    buf[0:8, :] = buf[tt:tt + 8, :]


def dn_conv(p, conv_w, *, batch, seq_pad, tt, tc):
    M = p.shape[0]
    nt = seq_pad // tt
    width = conv_w.shape[1]
    cblk = P_DNQKV // tc
    return pl.pallas_call(
        functools.partial(_dnconv_kernel, tt=tt),
        grid=(batch, width // tc, nt),
        in_specs=[pl.BlockSpec((tt, tc), lambda b, c, t: (b * nt + t, cblk + c)),
                  pl.BlockSpec((DN_CONV, tc), lambda b, c, t: (0, c))],
        out_specs=pl.BlockSpec((tt, tc), lambda b, c, t: (b * nt + t, c)),
        out_shape=jax.ShapeDtypeStruct((M, width), F32),
        scratch_shapes=[pltpu.VMEM((tt + 8, tc), F32)],
        compiler_params=_cparams("parallel", "parallel", "arbitrary"),
        name="dn_conv_silu",
    )(p, conv_w.astype(F32))


def _dn_gates_kernel(ab_ref, alog_ref, dtb_ref, gcum_ref, beta_ref):
    ab = ab_ref[...]
    g = -jnp.exp(alog_ref[...]) * jax.nn.softplus(ab[:DN_HEADS] + dtb_ref[...])
    beta_ref[...] = jax.nn.sigmoid(ab[DN_HEADS:])
    lane = lax.broadcasted_iota(jnp.int32, g.shape, 1) % CHUNK
    s = 1
    while s < CHUNK:
        g = g + jnp.where(lane >= s, pltpu.roll(g, s, axis=1), 0.0)
        s *= 2
    gcum_ref[...] = g


def dn_gates(ab_t, a_log, dt_bias):
    B, _, Lp = ab_t.shape
    col = pl.BlockSpec((DN_HEADS, 1), lambda b: (0, 0))
    out = pl.BlockSpec((None, DN_HEADS, Lp), lambda b: (b, 0, 0))
    return pl.pallas_call(
        _dn_gates_kernel,
        grid=(B,),
        in_specs=[pl.BlockSpec((None, 2 * DN_HEADS, Lp), lambda b: (b, 0, 0)), col, col],
        out_specs=[out, out],
        out_shape=[jax.ShapeDtypeStruct((B, DN_HEADS, Lp), F32)] * 2,
        compiler_params=_cparams("parallel"),
        name="dn_gates",
    )(ab_t, a_log.astype(F32).reshape(DN_HEADS, 1), dt_bias.astype(F32).reshape(DN_HEADS, 1))


def _dot(a, b):
    return jnp.dot(a, b, preferred_element_type=F32)


def _dot_nt(a, b):
    return lax.dot_general(a, b, (((1,), (1,)), ((), ())), preferred_element_type=F32)


def _dot_tn(a, b):
    return lax.dot_general(a, b, (((0,), (0,)), ((), ())), preferred_element_type=F32)


def _l2(x):
    return x * lax.rsqrt(jnp.sum(x * x, axis=-1, keepdims=True) + EPS)


def _dn_prep_kernel(q_ref, k_ref, v_ref, bcol_ref, gcol_ref, grow_ref,
                    u_ref, w_ref, qg_ref, kd_ref, qk_ref, dl_ref, *, n_chunks, unroll):
    C = CHUNK
    ii = lax.broadcasted_iota(jnp.int32, (C, C), 0)
    jj = lax.broadcasted_iota(jnp.int32, (C, C), 1)
    tri = ii >= jj
    strict = ii > jj
    eye = (ii == jj).astype(F32)

    def body(i, carry):
        U = range(unroll)
        cs = [i * unroll + j for j in U]
        sls = [pl.ds(pl.multiple_of(c * C, C), C) for c in cs]
        q = [_l2(q_ref[sl, :]) * (DN_DK ** -0.5) for sl in sls]
        k = [_l2(k_ref[sl, :]) for sl in sls]
        beta = [bcol_ref[sl, :] for sl in sls]
        gc = [gcol_ref[sl, :] for sl in sls]
        decay = [jnp.exp(jnp.where(tri, gc[j] - grow_ref[pl.ds(cs[j], 1), :], -jnp.inf)) for j in U]
        kb = [k[j] * beta[j] for j in U]
        a_mat = [jnp.where(strict, _dot_nt(kb[j], k[j]) * decay[j], 0.0) for j in U]
        t_inv = [eye - a for a in a_mat]
        pw = a_mat
        for _ in range(5):
            pw = [_dot(x, x) for x in pw]
            t_inv = [t + _dot(t, x) for t, x in zip(t_inv, pw)]
        u = [_dot(t_inv[j], v_ref[sls[j], :] * beta[j]) for j in U]
        w = [_dot(t_inv[j], kb[j] * jnp.exp(gc[j])) for j in U]
        qk = [_dot_nt(q[j].astype(BF16), k[j].astype(BF16)) * decay[j] for j in U]
        for j in U:
            sl = sls[j]
            u_ref[sl, :] = u[j]
            w_ref[sl, :] = w[j].astype(w_ref.dtype)
            qk_ref[sl, :] = qk[j].astype(qk_ref.dtype)
            qg_ref[sl, :] = (q[j] * jnp.exp(gc[j])).astype(qg_ref.dtype)
            g_last = gc[j][C - 1:C, :]
            kd_ref[sl, :] = (k[j] * jnp.exp(g_last - gc[j])).astype(kd_ref.dtype)
            dl_ref[pl.ds(cs[j], 1), :] = jnp.broadcast_to(jnp.exp(g_last), (1, dl_ref.shape[1]))
        return carry

    lax.fori_loop(0, n_chunks // unroll, body, 0)


def dn_prep(qkv, beta_col, gcum_col, gcum_row, *, batch, seq_pad):
    M = qkv.shape[0]
    H = DN_HEADS
    nc = seq_pad // CHUNK

    def head(off):
        return pl.BlockSpec((seq_pad, 128), lambda b, h: (b, off + h))

    colspec = pl.BlockSpec((None, None, seq_pad, 1), lambda b, h: (b, h, 0, 0))
    wide = jax.ShapeDtypeStruct((M, DN_WIDTH), BF16)
    return pl.pallas_call(
        functools.partial(_dn_prep_kernel, n_chunks=nc, unroll=_pick(nc, (11, 6, 4, 3, 2, 1))),
        grid=(batch, H),
        in_specs=[head(0), head(H), head(2 * H), colspec, colspec,
                  pl.BlockSpec((None, None, nc, CHUNK), lambda b, h: (b, h, 0, 0))],
        out_specs=[head(0), head(0), head(0), head(0),
                   pl.BlockSpec((None, None, seq_pad, CHUNK), lambda b, h: (b, h, 0, 0)),
                   pl.BlockSpec((None, None, nc, 128), lambda b, h: (b, h, 0, 0))],
        out_shape=[jax.ShapeDtypeStruct((M, DN_WIDTH), F32), wide, wide, wide,
                   jax.ShapeDtypeStruct((batch, H, seq_pad, CHUNK), BF16),
                   jax.ShapeDtypeStruct((batch, H, nc, 128), F32)],
        compiler_params=_cparams("parallel", "parallel"),
        name="dn_prep",
    )(qkv, qkv, qkv, beta_col, gcum_col, gcum_row)


def _dn_scan_kernel(u_ref, w_ref, qg_ref, kd_ref, qk_ref, dl_ref, z_ref, gn_ref, o_ref, s_ref,
                    *, chunks_per_tile):
    C = CHUNK

    @pl.when(pl.program_id(1) == 0)
    def _():
        s_ref[...] = jnp.zeros_like(s_ref)

    gn = gn_ref[...]

    def body(c, carry):
        sl = pl.ds(pl.multiple_of(c * C, C), C)
        H = range(DN_HEADS)
        hs = [slice(j * DN_DV, (j + 1) * DN_DV) for j in H]
        S = [s_ref[j] for j in H]
        Sb = [s.astype(BF16) for s in S]
        v_new = [u_ref[sl, hs[j]] - _dot(w_ref[sl, hs[j]], Sb[j]) for j in H]
        o_inter = [_dot(qg_ref[sl, hs[j]], Sb[j]) for j in H]
        vb = [v.astype(BF16) for v in v_new]
        o = [o_inter[j] + _dot(qk_ref[j, sl, :], vb[j]) for j in H]
        s_upd = [_dot_tn(kd_ref[sl, hs[j]], vb[j]) for j in H]
        for j in H:
            s_ref[j] = S[j] * dl_ref[j, pl.ds(c, 1), :] + s_upd[j]
            o_ref[sl, hs[j]] = (_rms(o[j], gn) * jax.nn.silu(z_ref[sl, hs[j]])).astype(o_ref.dtype)
        return carry

    lax.fori_loop(0, chunks_per_tile, body, 0)


def dn_scan(u, w, qg, kd, qk, dl, p, out_norm_g, *, batch, seq_pad):
    M = p.shape[0]
    H = DN_HEADS
    nc = seq_pad // CHUNK
    cpt = _pick(nc, (11, 6, 4, 3, 2, 1))
    nt = nc // cpt
    tt = cpt * CHUNK
    dl = dl.reshape(batch, H, nt, cpt, 128)

    def rows(col_block=0):
        return pl.BlockSpec((tt, DN_WIDTH), lambda b, t: (b * nt + t, col_block))

    return pl.pallas_call(
        functools.partial(_dn_scan_kernel, chunks_per_tile=cpt),
        grid=(batch, nt),
        in_specs=[rows(), rows(), rows(), rows(),
                  pl.BlockSpec((None, H, tt, CHUNK), lambda b, t: (b, 0, t, 0)),
                  pl.BlockSpec((None, H, None, cpt, 128), lambda b, t: (b, 0, t, 0, 0)),
                  rows(P_DNZ // DN_WIDTH),
                  pl.BlockSpec((1, DN_DV), lambda b, t: (0, 0))],
        out_specs=rows(),
        out_shape=jax.ShapeDtypeStruct((M, DN_WIDTH), BF16),
        scratch_shapes=[pltpu.VMEM((H, DN_DK, DN_DV), F32)],
        compiler_params=_cparams("parallel", "arbitrary"),
        name="dn_scan",
    )(u, w, qg, kd, qk, dl, p, out_norm_g.astype(F32).reshape(1, DN_DV))


def _rot_cols(w):
    half = w.shape[-1] // 2
    return jnp.concatenate([-w[..., half:], w[..., :half]], axis=-1)


def _prep_mixer(w_in, mla_w_uq, mla_w_ukv):
    depth = w_in.shape[0]
    wt = jnp.swapaxes(w_in, 1, 2)
    kr = wt[:, OFF_KR:OFF_S5]
    kr_rot = jnp.concatenate([-kr[:, MLA_ROPE // 2:], kr[:, :MLA_ROPE // 2]], axis=1)
    w_small = jnp.concatenate([
        wt[:, OFF_Q:OFF_KV], wt[:, OFF_DN_Z:OFF_DN_A], wt[:, OFF_S5:OFF_DN_Z],
        wt[:, OFF_KV:OFF_KR], kr, kr_rot, wt[:, OFF_DN_A:OFF_GATE],
        jnp.zeros((depth, 128 - 2 * DN_HEADS, D_MODEL), w_in.dtype)], axis=1).astype(BF16)
    w_gate = wt[:, OFF_GATE:].astype(BF16)
    uq = mla_w_uq.reshape(depth, MLA_Q_LORA, MLA_HEADS, MLA_QK)
    w_uq = jnp.concatenate([uq, _rot_cols(uq[..., MLA_NOPE:])], axis=-1)
    w_uq = w_uq.reshape(depth, MLA_Q_LORA, MLA_HEADS * MLA_QPAD).astype(BF16)
    ukv = mla_w_ukv.reshape(depth, MLA_KV_LORA, MLA_HEADS, 2, 128)
    w_ukv = jnp.transpose(ukv, (0, 1, 3, 2, 4)).reshape(depth, MLA_KV_LORA, 2 * MLA_HEADS * 128)
    return w_small, w_gate, w_uq, w_ukv.astype(BF16)


def _rope_tables(seq_pad, batch):
    inv = ROPE_THETA ** (-jnp.arange(0, MLA_ROPE, 2, dtype=F32) / MLA_ROPE)
    ang = jnp.arange(seq_pad, dtype=F32)[:, None] * inv[None, :]
    zeros = jnp.zeros((seq_pad, 128 - MLA_ROPE), F32)
    cos_t = jnp.concatenate([jnp.cos(ang), jnp.cos(ang), zeros], axis=1)
    sin_t = jnp.concatenate([jnp.sin(ang), jnp.sin(ang), zeros], axis=1)
    return jnp.tile(cos_t, (batch, 1)), jnp.tile(sin_t, (batch, 1))


def _ffn(xn, w13, w2, layer, *, tm):
    hid = matmul_swiglu(xn, w13, layer, tm=tm, tn=256)
    return matmul(hid, w2, layer, tm=tm // 2, tn=512, tk=FFN_HIDDEN, out_dtype=F32, name="ffn_down")


def kernel(x, meta_tokens, sandwich_g, ffn1_w13, ffn1_w2, w_in, mla_q_norm_g, mla_kv_norm_g, mla_w_uq, mla_w_ukv, mla_w_o, s5_a_re, s5_a_im, s5_log_dt, s5_b_re, s5_b_im, s5_c_re, s5_c_im, s5_d, s5_w_glu, dn_conv_w, dn_a_log, dn_dt_bias, dn_out_norm_g, dn_w_o, w_out, ffn2_w13, ffn2_w2):
    B, seq, D = x.shape
    depth = w_in.shape[0]
    L = N_META + seq
    assert (L - N_META) % CHUNK == 0, "pad keys are hidden by the chunk mask only on a chunk boundary"
    Lp = -(-L // 128) * 128
    M = B * Lp
    tm = _pick(M, (1056, 768, 384, 128))
    tr = _pick(M, (176, 128))
    tt = _pick(Lp, (528, 384, 128))
    ta = _pick(Lp, tuple(c for c in (384, 128) if c + 128 <= Lp))

    meta = jnp.broadcast_to(meta_tokens[None].astype(x.dtype), (B, N_META, D))
    h = jnp.concatenate([meta, x, jnp.zeros((B, Lp - L, D), x.dtype)], axis=1).reshape(M, D)
    cos_t, sin_t = _rope_tables(Lp, B)

    f1_w13, f1_w2 = ffn1_w13, ffn1_w2.astype(BF16)
    f2_w13, f2_w2 = ffn2_w13, ffn2_w2.astype(BF16)
    w_small, w_gate, w_uq, w_ukv = _prep_mixer(w_in, mla_w_uq, mla_w_ukv)
    w_mla_o, w_glu, w_dn_o, w_mix = (t.astype(BF16) for t in (mla_w_o, s5_w_glu, dn_w_o, w_out))
    q_scale = MLA_QK ** -0.5 * math.log2(math.e)

    xn = rmsnorm_cast(h, sandwich_g[0, 0], col_block=0, width=D, tm=tr)
    for l in range(depth):
        g = sandwich_g[l]
        y = _ffn(xn, f1_w13, f1_w2, l, tm=tm)
        h, xn = resid_norm(h, y, g[1], g[2], coef=0.5, tm=tr)

        p = matmul(xn, w_small, l, tm=tm, tn=768, tk=D, out_dtype=F32, w_transposed=True,
                   name="in_proj_small")
        gates = matmul(xn, w_gate, l, tm=tm, tn=1024, tk=D, out_dtype=BF16, act="sigmoid",
                       w_transposed=True, name="in_proj_gates")

        qn = rmsnorm_cast(p, mla_q_norm_g[l], col_block=P_Q // MLA_Q_LORA, width=MLA_Q_LORA, tm=tm)
        kvn = rmsnorm_cast(p, mla_kv_norm_g[l], col_block=P_KV // MLA_KV_LORA, width=MLA_KV_LORA, tm=tm)
        q = matmul_qrope(qn, w_uq, l, cos_t, sin_t, tm=tm, heads_per_tile=4, scale=q_scale)
        kv = matmul(kvn, w_ukv, l, tm=tm, tn=1024, tk=MLA_KV_LORA, out_dtype=BF16, name="mla_kv_proj")
        kr = k_rope(p, cos_t, sin_t, tm=tm)
        o_mla = mla_attention(q, kv, kr, batch=B, seq_pad=Lp, tq=ta)

        s5c = s5_constants(s5_a_re[l], s5_a_im[l], s5_log_dt[l], s5_b_re[l], s5_b_im[l],
                           s5_c_re[l], s5_c_im[l], s5_d[l])
        h_s5 = s5_scan(p, s5c, batch=B, seq_pad=Lp, tt=tt)

        qkv = dn_conv(p, dn_conv_w[l], batch=B, seq_pad=Lp, tt=tt, tc=512)
        ab_t = jnp.transpose(p[:, P_AB:P_AB + 2 * DN_HEADS].reshape(B, Lp, 2 * DN_HEADS), (0, 2, 1))
        gcum, beta = dn_gates(ab_t, dn_a_log[l], dn_dt_bias[l])
        u, w, qg, kd, qk, dl = dn_prep(qkv, beta[..., None], gcum[..., None],
                                       gcum.reshape(B, DN_HEADS, Lp // CHUNK, CHUNK),
                                       batch=B, seq_pad=Lp)
        o_dn = dn_scan(u, w, qg, kd, qk, dl, p, dn_out_norm_g[l], batch=B, seq_pad=Lp)

        merged = merge_branches(o_mla, h_s5, o_dn, w_mla_o, w_glu, w_dn_o, gates, l, tm=tm, tn=512)
        mix = matmul(merged, w_mix, l, tm=tm, tn=1024, tk=D, out_dtype=F32, name="out_proj")
        h, xn = resid_norm(h, mix, g[3], g[4], coef=1.0, tm=tr)

        y = _ffn(xn, f2_w13, f2_w2, l, tm=tm)
        g_next = sandwich_g[l + 1, 0] if l + 1 < depth else None
        h, xn = resid_norm(h, y, g[5], g_next, coef=0.5, tm=tr)
    return h.reshape(B, Lp, D)[:, N_META:L]
```

```python
import functools
import math

import jax
import jax.numpy as jnp
from jax import lax
from jax.experimental import pallas as pl
from jax.experimental.pallas import tpu as pltpu

F32 = jnp.float32
BF16 = jnp.bfloat16

D_MODEL = 4096
CHUNK = 64
N_META = 16
EPS = 1e-6

MLA_HEADS = 16
MLA_Q_LORA = 1024
MLA_KV_LORA = 512
MLA_NOPE = 128
MLA_ROPE = 64
MLA_V = 128
MLA_QK = MLA_NOPE + MLA_ROPE
MLA_WIDTH = MLA_HEADS * MLA_V
MLA_QPAD = 256
ROPE_THETA = 10000.0

S5_WIDTH = 1024
S5_GROUP = 16
S5_GROUPS = S5_WIDTH // S5_GROUP
S5_STATE = 64
S5_GB = 4
S5_GPB = S5_GROUPS // S5_GB

DN_HEADS = 8
DN_DK = 128
DN_DV = 128
DN_QK = DN_HEADS * DN_DK
DN_WIDTH = DN_HEADS * DN_DV
DN_CONV = 4

FFN_HIDDEN = 11008

OFF_Q = 0
OFF_KV = OFF_Q + MLA_Q_LORA
OFF_KR = OFF_KV + MLA_KV_LORA
OFF_S5 = OFF_KR + MLA_ROPE
OFF_DN_QKV = OFF_S5 + S5_WIDTH
OFF_DN_Z = OFF_DN_QKV + 2 * DN_QK + DN_WIDTH
OFF_DN_A = OFF_DN_Z + DN_WIDTH
OFF_DN_B = OFF_DN_A + DN_HEADS
OFF_GATE = OFF_DN_B + DN_HEADS

P_Q = 0
P_DNZ = P_Q + MLA_Q_LORA
P_S5 = P_DNZ + DN_WIDTH
P_DNQKV = P_S5 + S5_WIDTH
P_KV = P_DNQKV + 3 * DN_QK
P_KR = P_KV + MLA_KV_LORA
P_AB = P_KR + 128
P_WIDTH = P_AB + 128

VMEM_LIMIT_BYTES = 56 * 1024 * 1024


def _cparams(*sem):
    return pltpu.CompilerParams(dimension_semantics=sem, vmem_limit_bytes=VMEM_LIMIT_BYTES)


def _pick(n, candidates):
    for c in candidates:
        if n % c == 0:
            return c
    raise ValueError(f"no tile in {candidates} divides {n}")


def _rms(x, g):
    return x * lax.rsqrt(jnp.mean(x * x, axis=-1, keepdims=True) + EPS) * g


def _rmsnorm_kernel(x_ref, g_ref, o_ref):
    o_ref[...] = _rms(x_ref[...], g_ref[...]).astype(o_ref.dtype)


def rmsnorm_cast(x, g, *, col_block, width, tm):
    M = x.shape[0]
    return pl.pallas_call(
        _rmsnorm_kernel,
        grid=(M // tm,),
        in_specs=[pl.BlockSpec((tm, width), lambda i: (i, col_block)),
                  pl.BlockSpec((1, width), lambda i: (0, 0))],
        out_specs=pl.BlockSpec((tm, width), lambda i: (i, 0)),
        out_shape=jax.ShapeDtypeStruct((M, width), BF16),
        compiler_params=_cparams("parallel"),
        name="rmsnorm_cast",
    )(x, g.reshape(1, width).astype(F32))


def _resid_norm_kernel(h_ref, y_ref, gpost_ref, gpre_ref, ho_ref, xn_ref, *, coef):
    h = h_ref[...] + coef * _rms(y_ref[...].astype(F32), gpost_ref[...])
    ho_ref[...] = h
    xn_ref[...] = _rms(h, gpre_ref[...]).astype(xn_ref.dtype)


def _resid_kernel(h_ref, y_ref, gpost_ref, ho_ref, *, coef):
    ho_ref[...] = h_ref[...] + coef * _rms(y_ref[...].astype(F32), gpost_ref[...])


def resid_norm(h, y, g_post, g_pre, *, coef, tm):
    M, D = h.shape
    row = pl.BlockSpec((tm, D), lambda i: (i, 0))
    gain = pl.BlockSpec((1, D), lambda i: (0, 0))
    if g_pre is None:
        return pl.pallas_call(
            functools.partial(_resid_kernel, coef=coef),
            grid=(M // tm,), in_specs=[row, row, gain], out_specs=row,
            out_shape=jax.ShapeDtypeStruct((M, D), F32),
            input_output_aliases={0: 0},
            compiler_params=_cparams("parallel"), name="resid",
        )(h, y, g_post.reshape(1, D)), None
    return pl.pallas_call(
        functools.partial(_resid_norm_kernel, coef=coef),
        grid=(M // tm,), in_specs=[row, row, gain, gain], out_specs=[row, row],
        out_shape=[jax.ShapeDtypeStruct((M, D), F32), jax.ShapeDtypeStruct((M, D), BF16)],
        input_output_aliases={0: 0},
        compiler_params=_cparams("parallel"), name="resid_norm",
    )(h, y, g_post.reshape(1, D), g_pre.reshape(1, D))


def _mm_kernel(x_ref, w_ref, o_ref, *, act, w_transposed):
    if w_transposed:
        acc = lax.dot_general(x_ref[...], w_ref[...], (((1,), (1,)), ((), ())),
                              preferred_element_type=F32)
    else:
        acc = jnp.dot(x_ref[...], w_ref[...], preferred_element_type=F32)
    if act == "sigmoid":
        acc = jax.nn.sigmoid(acc)
    o_ref[...] = acc.astype(o_ref.dtype)


def matmul(x, w, layer, *, tm, tn, out_dtype, act=None, w_transposed=False,
           single_buffer_x=False, name="matmul"):
    M, K = x.shape
    N = w.shape[1] if w_transposed else w.shape[2]
    x_mode = {"pipeline_mode": pl.Buffered(1)} if single_buffer_x else {}
    if w_transposed:
        w_spec = pl.BlockSpec((None, tn, K), lambda i, j: (layer, j, 0))
    else:
        w_spec = pl.BlockSpec((None, K, tn), lambda i, j: (layer, 0, j))
    return pl.pallas_call(
        functools.partial(_mm_kernel, act=act, w_transposed=w_transposed),
        grid=(M // tm, N // tn),
        in_specs=[pl.BlockSpec((tm, K), lambda i, j: (i, 0), **x_mode), w_spec],
        out_specs=pl.BlockSpec((tm, tn), lambda i, j: (i, j)),
        out_shape=jax.ShapeDtypeStruct((M, N), out_dtype),
        compiler_params=_cparams("parallel", "parallel"),
        name=name,
    )(x, w)


def _mm_swiglu_kernel(x_ref, wa_ref, wb_ref, o_ref):
    x = x_ref[...]
    a = jnp.dot(x, wa_ref[...].astype(BF16), preferred_element_type=F32)
    b = jnp.dot(x, wb_ref[...].astype(BF16), preferred_element_type=F32)
    o_ref[...] = (jax.nn.silu(a) * b).astype(o_ref.dtype)


def matmul_swiglu(x, w13, layer, *, tm, tn):
    M, K = x.shape
    N = w13.shape[2] // 2
    nj = N // tn
    return pl.pallas_call(
        _mm_swiglu_kernel,
        grid=(M // tm, nj),
        in_specs=[pl.BlockSpec((tm, K), lambda i, j: (i, 0), pipeline_mode=pl.Buffered(1)),
                  pl.BlockSpec((None, K, tn), lambda i, j: (layer, 0, j)),
                  pl.BlockSpec((None, K, tn), lambda i, j: (layer, 0, nj + j))],
        out_specs=pl.BlockSpec((tm, tn), lambda i, j: (i, j)),
        out_shape=jax.ShapeDtypeStruct((M, N), BF16),
        compiler_params=_cparams("parallel", "parallel"),
        name="ffn_up_swiglu",
    )(x, w13, w13)


def _rope_block(a, c, s):
    return a * c + pltpu.roll(a, 64, axis=1) * s


def _mm_qrope_kernel(x_ref, w_ref, c_ref, s_ref, o_ref, *, heads, scale):
    acc = jnp.dot(x_ref[...], w_ref[...], preferred_element_type=F32) * scale
    c = c_ref[...]
    s = s_ref[...]
    for hb in range(heads):
        lo = hb * MLA_QPAD
        o_ref[:, lo:lo + 128] = acc[:, lo:lo + 128].astype(o_ref.dtype)
        o_ref[:, lo + 128:lo + 256] = _rope_block(acc[:, lo + 128:lo + 256], c, s).astype(o_ref.dtype)


def matmul_qrope(x, w, layer, cos_t, sin_t, *, tm, heads_per_tile, scale):
    M, K = x.shape
    N = w.shape[2]
    tn = heads_per_tile * MLA_QPAD
    tab = pl.BlockSpec((tm, 128), lambda i, j: (i, 0))
    return pl.pallas_call(
        functools.partial(_mm_qrope_kernel, heads=heads_per_tile, scale=scale),
        grid=(M // tm, N // tn),
        in_specs=[pl.BlockSpec((tm, K), lambda i, j: (i, 0)),
                  pl.BlockSpec((None, K, tn), lambda i, j: (layer, 0, j)), tab, tab],
        out_specs=pl.BlockSpec((tm, tn), lambda i, j: (i, j)),
        out_shape=jax.ShapeDtypeStruct((M, N), BF16),
        compiler_params=_cparams("parallel", "parallel"),
        name="mla_q_proj_rope",
    )(x, w, cos_t, sin_t)


def _krope_kernel(x_ref, c_ref, s_ref, o_ref):
    o_ref[...] = _rope_block(x_ref[...], c_ref[...], s_ref[...]).astype(o_ref.dtype)


def k_rope(p, cos_t, sin_t, *, tm):
    M = p.shape[0]
    tab = pl.BlockSpec((tm, 128), lambda i: (i, 0))
    return pl.pallas_call(
        _krope_kernel,
        grid=(M // tm,),
        in_specs=[pl.BlockSpec((tm, 128), lambda i: (i, P_KR // 128)), tab, tab],
        out_specs=tab,
        out_shape=jax.ShapeDtypeStruct((M, 128), BF16),
        compiler_params=_cparams("parallel"),
        name="mla_k_rope",
    )(p, cos_t, sin_t)


def _merge_kernel(o_ref, hs_ref, dn_ref, wo_ref, wv_ref, wg_ref, wd_ref,
                  g0_ref, g1_ref, g2_ref, out_ref):
    hs = hs_ref[...]
    y_mla = jnp.dot(o_ref[...], wo_ref[...], preferred_element_type=F32)
    val = jnp.dot(hs, wv_ref[...], preferred_element_type=F32)
    gate = jnp.dot(hs, wg_ref[...], preferred_element_type=F32)
    y_dn = jnp.dot(dn_ref[...], wd_ref[...], preferred_element_type=F32)
    merged = (g0_ref[...].astype(F32) * y_mla
              + g1_ref[...].astype(F32) * (val * jax.nn.sigmoid(gate))
              + g2_ref[...].astype(F32) * y_dn)
    out_ref[...] = merged.astype(out_ref.dtype)


def merge_branches(o_mla, h_s5, o_dn, w_o, w_glu, w_dn_o, gates, layer, *, tm, tn):
    M = o_mla.shape[0]
    D = w_o.shape[2]
    nj = D // tn

    def rows(width):
        return pl.BlockSpec((tm, width), lambda i, j: (i, 0))

    def cols(kdim, off):
        return pl.BlockSpec((None, kdim, tn), lambda i, j: (layer, 0, off * nj + j))

    def gate(b):
        return pl.BlockSpec((tm, tn), lambda i, j: (i, b * nj + j))

    return pl.pallas_call(
        _merge_kernel,
        grid=(M // tm, nj),
        in_specs=[rows(MLA_WIDTH), rows(S5_WIDTH), rows(DN_WIDTH),
                  cols(MLA_WIDTH, 0), cols(S5_WIDTH, 0), cols(S5_WIDTH, 1), cols(DN_WIDTH, 0),
                  gate(0), gate(1), gate(2)],
        out_specs=pl.BlockSpec((tm, tn), lambda i, j: (i, j)),
        out_shape=jax.ShapeDtypeStruct((M, D), BF16),
        compiler_params=_cparams("parallel", "parallel"),
        name="merge_branches",
    )(o_mla, h_s5, o_dn, w_o, w_glu, w_glu, w_dn_o, gates, gates, gates)


_NEG = -1e30


def _chunk_id(pos):
    return (pos + (CHUNK - N_META)) >> 6


ATTN_HEADS_PER_STEP = 4


def _attn_kernel(q_ref, kn_ref, kr_ref, v_ref, o_ref, *, tq, seq_pad):
    G = ATTN_HEADS_PER_STEP
    tk = tq
    tkm = tq + 128
    q0 = pl.program_id(2) * tq
    qs = [q_ref[:, g * MLA_QPAD:(g + 1) * MLA_QPAD] for g in range(G)]
    full_end = (_chunk_id(q0) + 1) * CHUNK - (CHUNK - N_META)
    n_full = full_end // tk

    def step(ks, width, carry, mask):
        ms, accs = carry
        k_rope = kr_ref[pl.ds(ks, width), :]
        ones = jnp.ones((width, 128), BF16)

        def scores(g):
            k = jnp.concatenate([kn_ref[pl.ds(ks, width), g * 128:(g + 1) * 128], k_rope], axis=1)
            s = lax.dot_general(qs[g], k, (((1,), (1,)), ((), ())), preferred_element_type=F32)
            return s if mask is None else jnp.where(mask, s, _NEG)

        def weights(g, s):
            m_new = jnp.maximum(ms[g], jnp.max(s, axis=-1, keepdims=True))
            return m_new, jnp.exp2(s - m_new).astype(BF16), jnp.exp2(ms[g] - m_new)

        def update(g, p, alpha):
            v1 = jnp.concatenate([v_ref[pl.ds(ks, width), g * 128:(g + 1) * 128], ones], axis=1)
            return alpha * accs[g] + jnp.dot(p, v1, preferred_element_type=F32)

        ss = [scores(g) for g in range(G)]
        m_new, ps, alphas = zip(*[weights(g, ss[g]) for g in range(G)])
        return list(m_new), [update(g, ps[g], alphas[g]) for g in range(G)]

    def full_step(j, carry):
        return step(pl.multiple_of(j * tk, tk), tk, carry, None)

    init = ([jnp.full((tq, 1), _NEG, F32)] * G, [jnp.zeros((tq, 2 * MLA_V), F32)] * G)
    carry = lax.fori_loop(0, n_full, full_step, init)

    first = n_full * tk
    ks = pl.multiple_of(jnp.minimum(first, seq_pad - tkm), 128)
    kpos = ks + lax.broadcasted_iota(jnp.int32, (1, tkm), 1)
    cq = _chunk_id(q0 + lax.broadcasted_iota(jnp.int32, (tq, 1), 0))
    mask = jnp.logical_and(cq >= _chunk_id(kpos), kpos >= first)
    _, accs = step(ks, tkm, carry, mask)
    for g in range(G):
        o_ref[:, g * MLA_V:(g + 1) * MLA_V] = (
            accs[g][:, :MLA_V] / accs[g][:, MLA_V:]).astype(o_ref.dtype)


def mla_attention(q, kv, kr, *, batch, seq_pad, tq):
    M = q.shape[0]
    nq = seq_pad // tq
    G = ATTN_HEADS_PER_STEP
    ng = MLA_HEADS // G
    assert tq + 128 <= seq_pad and tq > CHUNK
    return pl.pallas_call(
        functools.partial(_attn_kernel, tq=tq, seq_pad=seq_pad),
        grid=(batch, ng, nq),
        in_specs=[pl.BlockSpec((tq, G * MLA_QPAD), lambda b, h, i: (b * nq + i, h)),
                  pl.BlockSpec((seq_pad, G * 128), lambda b, h, i: (b, h)),
                  pl.BlockSpec((seq_pad, 128), lambda b, h, i: (b, 0)),
                  pl.BlockSpec((seq_pad, G * 128), lambda b, h, i: (b, ng + h))],
        out_specs=pl.BlockSpec((tq, G * MLA_V), lambda b, h, i: (b * nq + i, h)),
        out_shape=jax.ShapeDtypeStruct((M, MLA_WIDTH), BF16),
        compiler_params=_cparams("parallel", "parallel", "parallel"),
        name="mla_attention",
    )(q, kv, kr, kv)


def _s5_kernel(u_ref, br_ref, bi_ref, cr_ref, ci_ref, d_ref, ast_ref, pw_ref, o_ref,
               xr_s, xi_s, car_s, *, tt):
    n = S5_GPB * S5_STATE

    @pl.when(pl.program_id(2) == 0)
    def _():
        car_s[...] = jnp.zeros_like(car_s)

    u = u_ref[...]
    ub = u.astype(BF16)
    xr = jnp.dot(ub, br_ref[...], preferred_element_type=F32).reshape(tt // 8, 8, n)
    xi = jnp.dot(ub, bi_ref[...], preferred_element_type=F32).reshape(tt // 8, 8, n)
    for si in range(3):
        ar = ast_ref[2 * si]
        ai = ast_ref[2 * si + 1]
        sr = pltpu.roll(xr, 1 << si, axis=1)
        sm = pltpu.roll(xi, 1 << si, axis=1)
        xr, xi = xr + (ar * sr - ai * sm), xi + (ar * sm + ai * sr)
    xr_s[...] = xr.reshape(tt, n)
    xi_s[...] = xi.reshape(tt, n)
    pr = pw_ref[0]
    pi = pw_ref[1]

    def body(i, carry):
        cr, ci = carry
        sl = pl.ds(pl.multiple_of(i * 8, 8), 8)
        a = xr_s[sl, :] + (pr * cr - pi * ci)
        b = xi_s[sl, :] + (pr * ci + pi * cr)
        xr_s[sl, :] = a
        xi_s[sl, :] = b
        return (jnp.broadcast_to(a[7:8, :], (8, n)), jnp.broadcast_to(b[7:8, :], (8, n)))

    cr, ci = lax.fori_loop(0, tt // 8, body, (car_s[0], car_s[1]))
    car_s[0] = cr
    car_s[1] = ci
    y = (jnp.dot(xr_s[...].astype(BF16), cr_ref[...], preferred_element_type=F32)
         - jnp.dot(xi_s[...].astype(BF16), ci_ref[...], preferred_element_type=F32)
         + d_ref[...] * u)
    o_ref[...] = jax.nn.gelu(y).astype(o_ref.dtype)


def s5_scan(p, s5c, *, batch, seq_pad, tt):
    M = p.shape[0]
    nt = seq_pad // tt
    cw = S5_GPB * S5_GROUP
    n = S5_GPB * S5_STATE
    ublk = P_S5 // cw
    return pl.pallas_call(
        functools.partial(_s5_kernel, tt=tt),
        grid=(batch, S5_GB, nt),
        in_specs=[pl.BlockSpec((tt, cw), lambda b, g, t: (b * nt + t, ublk + g)),
                  pl.BlockSpec((None, cw, n), lambda b, g, t: (g, 0, 0)),
                  pl.BlockSpec((None, cw, n), lambda b, g, t: (g, 0, 0)),
                  pl.BlockSpec((None, n, cw), lambda b, g, t: (g, 0, 0)),
                  pl.BlockSpec((None, n, cw), lambda b, g, t: (g, 0, 0)),
                  pl.BlockSpec((None, 1, cw), lambda b, g, t: (g, 0, 0)),
                  pl.BlockSpec((None, 6, 8, n), lambda b, g, t: (g, 0, 0, 0)),
                  pl.BlockSpec((None, 2, 8, n), lambda b, g, t: (g, 0, 0, 0))],
        out_specs=pl.BlockSpec((tt, cw), lambda b, g, t: (b * nt + t, g)),
        out_shape=jax.ShapeDtypeStruct((M, S5_WIDTH), BF16),
        scratch_shapes=[pltpu.VMEM((tt, n), F32), pltpu.VMEM((tt, n), F32),
                        pltpu.VMEM((2, 8, n), F32)],
        compiler_params=_cparams("parallel", "parallel", "arbitrary"),
        name="s5_scan",
    )(p, s5c["br"], s5c["bi"], s5c["cr"], s5c["ci"], s5c["d"], s5c["ast"], s5c["pw"])


def s5_constants(a_re, a_im, log_dt, b_re, b_im, c_re, c_im, d):
    ar, ai = a_re.astype(F32), a_im.astype(F32)
    delta = jnp.exp(log_dt.astype(F32))[:, None]
    mag = jnp.exp(ar * delta)
    abar_r, abar_i = mag * jnp.cos(ai * delta), mag * jnp.sin(ai * delta)
    den = ar * ar + ai * ai
    zr = ((abar_r - 1.0) * ar + abar_i * ai) / den
    zi = (abar_i * ar - (abar_r - 1.0) * ai) / den
    br, bi = b_re.astype(F32), b_im.astype(F32)
    bbar_r = zr[..., None] * br - zi[..., None] * bi
    bbar_i = zr[..., None] * bi + zi[..., None] * br

    eye = jnp.eye(S5_GPB, dtype=F32)

    def in_blockdiag(m):
        m = m.reshape(S5_GB, S5_GPB, S5_STATE, S5_GROUP)
        bd = jnp.einsum("bgpc,gh->bgchp", m, eye)
        return bd.reshape(S5_GB, S5_GPB * S5_GROUP, S5_GPB * S5_STATE).astype(BF16)

    def out_blockdiag(m):
        m = m.reshape(S5_GB, S5_GPB, S5_GROUP, S5_STATE)
        bd = jnp.einsum("bgcp,gh->bgphc", m, eye)
        return bd.reshape(S5_GB, S5_GPB * S5_STATE, S5_GPB * S5_GROUP).astype(BF16)

    def cmul(x, y):
        return x[0] * y[0] - x[1] * y[1], x[0] * y[1] + x[1] * y[0]

    n = S5_GPB * S5_STATE
    a1 = (abar_r.reshape(S5_GB, n), abar_i.reshape(S5_GB, n))
    a2 = cmul(a1, a1)
    a4 = cmul(a2, a2)
    row = jnp.arange(8)[None, :, None]
    ast = []
    for s, a in ((1, a1), (2, a2), (4, a4)):
        for part in a:
            ast.append(jnp.where(row >= s, part[:, None, :], 0.0))
    ast = jnp.stack(ast, axis=1)
    pows = [a1]
    for _ in range(7):
        pows.append(cmul(pows[-1], a1))
    pw = jnp.stack([jnp.stack([p[0] for p in pows], axis=1),
                    jnp.stack([p[1] for p in pows], axis=1)], axis=1)
    return {
        "br": in_blockdiag(bbar_r), "bi": in_blockdiag(bbar_i),
        "cr": out_blockdiag(c_re.astype(F32)), "ci": out_blockdiag(c_im.astype(F32)),
        "d": d.astype(F32).reshape(S5_GB, 1, S5_GPB * S5_GROUP),
        "ast": ast, "pw": pw,
    }


def _dn_gates_kernel(ab_ref, alog_ref, dtb_ref, gcum_ref, beta_ref):
    ab = ab_ref[...]
    g = -jnp.exp(alog_ref[...]) * jax.nn.softplus(ab[:DN_HEADS] + dtb_ref[...])
    beta_ref[...] = jax.nn.sigmoid(ab[DN_HEADS:])
    lane = lax.broadcasted_iota(jnp.int32, g.shape, 1) % CHUNK
    s = 1
    while s < CHUNK:
        g = g + jnp.where(lane >= s, pltpu.roll(g, s, axis=1), 0.0)
        s *= 2
    gcum_ref[...] = g


def dn_gates(ab_t, a_log, dt_bias):
    B, _, Lp = ab_t.shape
    col = pl.BlockSpec((DN_HEADS, 1), lambda b: (0, 0))
    out = pl.BlockSpec((None, DN_HEADS, Lp), lambda b: (b, 0, 0))
    return pl.pallas_call(
        _dn_gates_kernel,
        grid=(B,),
        in_specs=[pl.BlockSpec((None, 2 * DN_HEADS, Lp), lambda b: (b, 0, 0)), col, col],
        out_specs=[out, out],
        out_shape=[jax.ShapeDtypeStruct((B, DN_HEADS, Lp), F32)] * 2,
        compiler_params=_cparams("parallel"),
        name="dn_gates",
    )(ab_t, a_log.astype(F32).reshape(DN_HEADS, 1), dt_bias.astype(F32).reshape(DN_HEADS, 1))


def _dot(a, b):
    return jnp.dot(a, b, preferred_element_type=F32)


def _dot_nt(a, b):
    return lax.dot_general(a, b, (((1,), (1,)), ((), ())), preferred_element_type=F32)


def _dot_tn(a, b):
    return lax.dot_general(a, b, (((0,), (0,)), ((), ())), preferred_element_type=F32)


def _l2(x):
    return x * lax.rsqrt(jnp.sum(x * x, axis=-1, keepdims=True) + EPS)


def _dn_prep_kernel(q_ref, k_ref, v_ref, cwq_ref, cwk_ref, cwv_ref, bcol_ref, gcol_ref, grow_ref,
                    u_ref, w_ref, qg_ref, kd_ref, qk_ref, dl_ref, *, n_chunks, unroll):
    C = CHUNK

    def conv_silu(x_ref, cw_ref, c, sl):
        cur = x_ref[sl, :]
        halo = x_ref[pl.ds(pl.multiple_of(jnp.maximum(c * C - 8, 0), 8), 8), :]
        xc = jnp.concatenate([jnp.where(c > 0, halo, 0.0), cur], axis=0)
        acc = cur * cw_ref[DN_CONV - 1:DN_CONV, :]
        for j in range(DN_CONV - 1):
            lo = 8 - (DN_CONV - 1) + j
            acc = acc + xc[lo:lo + C, :] * cw_ref[j:j + 1, :]
        return jax.nn.silu(acc)

    ii = lax.broadcasted_iota(jnp.int32, (C, C), 0)
    jj = lax.broadcasted_iota(jnp.int32, (C, C), 1)
    tri = ii >= jj
    strict = ii > jj
    eye = (ii == jj).astype(F32)

    def body(i, carry):
        U = range(unroll)
        cs = [i * unroll + j for j in U]
        sls = [pl.ds(pl.multiple_of(c * C, C), C) for c in cs]
        q = [_l2(conv_silu(q_ref, cwq_ref, cs[j], sls[j])) * (DN_DK ** -0.5) for j in U]
        k = [_l2(conv_silu(k_ref, cwk_ref, cs[j], sls[j])) for j in U]
        v = [conv_silu(v_ref, cwv_ref, cs[j], sls[j]) for j in U]
        beta = [bcol_ref[sl, :] for sl in sls]
        gc = [gcol_ref[sl, :] for sl in sls]
        decay = [jnp.exp(jnp.where(tri, gc[j] - grow_ref[pl.ds(cs[j], 1), :], -jnp.inf)) for j in U]
        kb = [k[j] * beta[j] for j in U]
        a_mat = [jnp.where(strict, _dot_nt(kb[j], k[j]) * decay[j], 0.0) for j in U]
        t_inv = [eye - a for a in a_mat]
        pw = a_mat
        for _ in range(5):
            pw = [_dot(x, x) for x in pw]
            t_inv = [t + _dot(t, x) for t, x in zip(t_inv, pw)]
        u = [_dot(t_inv[j], v[j] * beta[j]) for j in U]
        w = [_dot(t_inv[j], kb[j] * jnp.exp(gc[j])) for j in U]
        qk = [_dot_nt(q[j].astype(BF16), k[j].astype(BF16)) * decay[j] for j in U]
        for j in U:
            sl = sls[j]
            u_ref[sl, :] = u[j]
            w_ref[sl, :] = w[j].astype(w_ref.dtype)
            qk_ref[sl, :] = qk[j].astype(qk_ref.dtype)
            qg_ref[sl, :] = (q[j] * jnp.exp(gc[j])).astype(qg_ref.dtype)
            g_last = gc[j][C - 1:C, :]
            kd_ref[sl, :] = (k[j] * jnp.exp(g_last - gc[j])).astype(kd_ref.dtype)
            dl_ref[pl.ds(cs[j], 1), :] = jnp.broadcast_to(jnp.exp(g_last), (1, dl_ref.shape[1]))
        return carry

    lax.fori_loop(0, n_chunks // unroll, body, 0)


def dn_prep(p, conv_w, beta_col, gcum_col, gcum_row, *, batch, seq_pad):
    M = p.shape[0]
    H = DN_HEADS
    nc = seq_pad // CHUNK
    c0 = P_DNQKV // 128

    def head(off):
        return pl.BlockSpec((seq_pad, 128), lambda b, h: (b, off + h))

    def taps(off):
        return pl.BlockSpec((DN_CONV, 128), lambda b, h: (0, off + h))

    colspec = pl.BlockSpec((None, None, seq_pad, 1), lambda b, h: (b, h, 0, 0))
    wide = jax.ShapeDtypeStruct((M, DN_WIDTH), BF16)
    conv_w = conv_w.astype(F32)
    return pl.pallas_call(
        functools.partial(_dn_prep_kernel, n_chunks=nc, unroll=_pick(nc, (11, 6, 4, 3, 2, 1))),
        grid=(batch, H),
        in_specs=[head(c0), head(c0 + H), head(c0 + 2 * H), taps(0), taps(H), taps(2 * H),
                  colspec, colspec,
                  pl.BlockSpec((None, None, nc, CHUNK), lambda b, h: (b, h, 0, 0))],
        out_specs=[head(0), head(0), head(0), head(0),
                   pl.BlockSpec((None, None, seq_pad, CHUNK), lambda b, h: (b, h, 0, 0)),
                   pl.BlockSpec((None, None, nc, 128), lambda b, h: (b, h, 0, 0))],
        out_shape=[jax.ShapeDtypeStruct((M, DN_WIDTH), F32), wide, wide, wide,
                   jax.ShapeDtypeStruct((batch, H, seq_pad, CHUNK), BF16),
                   jax.ShapeDtypeStruct((batch, H, nc, 128), F32)],
        compiler_params=_cparams("parallel", "parallel"),
        name="dn_prep",
    )(p, p, p, conv_w, conv_w, conv_w, beta_col, gcum_col, gcum_row)


def _dn_scan_kernel(u_ref, w_ref, qg_ref, kd_ref, qk_ref, dl_ref, z_ref, gn_ref, o_ref, s_ref,
                    *, chunks_per_tile):
    C = CHUNK

    @pl.when(pl.program_id(1) == 0)
    def _():
        s_ref[...] = jnp.zeros_like(s_ref)

    gn = gn_ref[...]

    def body(c, carry):
        sl = pl.ds(pl.multiple_of(c * C, C), C)
        H = range(DN_HEADS)
        hs = [slice(j * DN_DV, (j + 1) * DN_DV) for j in H]
        S = [s_ref[j] for j in H]
        Sb = [s.astype(BF16) for s in S]
        v_new = [u_ref[sl, hs[j]] - _dot(w_ref[sl, hs[j]], Sb[j]) for j in H]
        o_inter = [_dot(qg_ref[sl, hs[j]], Sb[j]) for j in H]
        vb = [v.astype(BF16) for v in v_new]
        o = [o_inter[j] + _dot(qk_ref[j, sl, :], vb[j]) for j in H]
        s_upd = [_dot_tn(kd_ref[sl, hs[j]], vb[j]) for j in H]
        for j in H:
            s_ref[j] = S[j] * dl_ref[j, pl.ds(c, 1), :] + s_upd[j]
            o_ref[sl, hs[j]] = (_rms(o[j], gn) * jax.nn.silu(z_ref[sl, hs[j]])).astype(o_ref.dtype)
        return carry

    lax.fori_loop(0, chunks_per_tile, body, 0)


def dn_scan(u, w, qg, kd, qk, dl, p, out_norm_g, *, batch, seq_pad):
    M = p.shape[0]
    H = DN_HEADS
    nc = seq_pad // CHUNK
    cpt = _pick(nc, (11, 6, 4, 3, 2, 1))
    nt = nc // cpt
    tt = cpt * CHUNK
    dl = dl.reshape(batch, H, nt, cpt, 128)

    def rows(col_block=0):
        return pl.BlockSpec((tt, DN_WIDTH), lambda b, t: (b * nt + t, col_block))

    return pl.pallas_call(
        functools.partial(_dn_scan_kernel, chunks_per_tile=cpt),
        grid=(batch, nt),
        in_specs=[rows(), rows(), rows(), rows(),
                  pl.BlockSpec((None, H, tt, CHUNK), lambda b, t: (b, 0, t, 0)),
                  pl.BlockSpec((None, H, None, cpt, 128), lambda b, t: (b, 0, t, 0, 0)),
                  rows(P_DNZ // DN_WIDTH),
                  pl.BlockSpec((1, DN_DV), lambda b, t: (0, 0))],
        out_specs=rows(),
        out_shape=jax.ShapeDtypeStruct((M, DN_WIDTH), BF16),
        scratch_shapes=[pltpu.VMEM((H, DN_DK, DN_DV), F32)],
        compiler_params=_cparams("parallel", "arbitrary"),
        name="dn_scan",
    )(u, w, qg, kd, qk, dl, p, out_norm_g.astype(F32).reshape(1, DN_DV))


def _rot_cols(w):
    half = w.shape[-1] // 2
    return jnp.concatenate([-w[..., half:], w[..., :half]], axis=-1)


def _prep_mixer(w_in, mla_w_uq, mla_w_ukv):
    depth = w_in.shape[0]
    wt = jnp.swapaxes(w_in, 1, 2)

    def rows(lo, hi):
        return wt[:, lo:hi].astype(BF16)

    kr = rows(OFF_KR, OFF_S5)
    kr_rot = jnp.concatenate([-kr[:, MLA_ROPE // 2:], kr[:, :MLA_ROPE // 2]], axis=1)
    w_small = jnp.concatenate([
        rows(OFF_Q, OFF_KV), rows(OFF_DN_Z, OFF_DN_A), rows(OFF_S5, OFF_DN_Z),
        rows(OFF_KV, OFF_KR), kr, kr_rot, rows(OFF_DN_A, OFF_GATE),
        jnp.zeros((depth, 128 - 2 * DN_HEADS, D_MODEL), BF16)], axis=1)
    w_gate = rows(OFF_GATE, wt.shape[1])
    uq = mla_w_uq.reshape(depth, MLA_Q_LORA, MLA_HEADS, MLA_QK)
    w_uq = jnp.concatenate([uq, _rot_cols(uq[..., MLA_NOPE:])], axis=-1)
    w_uq = w_uq.reshape(depth, MLA_Q_LORA, MLA_HEADS * MLA_QPAD).astype(BF16)
    ukv = mla_w_ukv.reshape(depth, MLA_KV_LORA, MLA_HEADS, 2, 128)
    w_ukv = jnp.transpose(ukv, (0, 1, 3, 2, 4)).reshape(depth, MLA_KV_LORA, 2 * MLA_HEADS * 128)
    return w_small, w_gate, w_uq, w_ukv.astype(BF16)


def _rope_tables(seq_pad, batch):
    inv = ROPE_THETA ** (-jnp.arange(0, MLA_ROPE, 2, dtype=F32) / MLA_ROPE)
    ang = jnp.arange(seq_pad, dtype=F32)[:, None] * inv[None, :]
    zeros = jnp.zeros((seq_pad, 128 - MLA_ROPE), F32)
    cos_t = jnp.concatenate([jnp.cos(ang), jnp.cos(ang), zeros], axis=1)
    sin_t = jnp.concatenate([jnp.sin(ang), jnp.sin(ang), zeros], axis=1)
    return jnp.tile(cos_t, (batch, 1)), jnp.tile(sin_t, (batch, 1))


def _ffn(xn, w13, w2, layer, *, tm_up, tm):
    hid = matmul_swiglu(xn, w13, layer, tm=tm_up, tn=256)
    return matmul(hid, w2, layer, tm=tm, tn=512, out_dtype=BF16, single_buffer_x=True,
                  name="ffn_down")


def kernel(x, meta_tokens, sandwich_g, ffn1_w13, ffn1_w2, w_in, mla_q_norm_g, mla_kv_norm_g, mla_w_uq, mla_w_ukv, mla_w_o, s5_a_re, s5_a_im, s5_log_dt, s5_b_re, s5_b_im, s5_c_re, s5_c_im, s5_d, s5_w_glu, dn_conv_w, dn_a_log, dn_dt_bias, dn_out_norm_g, dn_w_o, w_out, ffn2_w13, ffn2_w2):
    B, seq, D = x.shape
    depth = w_in.shape[0]
    L = N_META + seq
    assert (L - N_META) % CHUNK == 0, "pad keys are hidden by the chunk mask only on a chunk boundary"
    Lp = -(-L // 128) * 128
    M = B * Lp
    tm = _pick(M, (1056, 768, 384, 128))
    tm_up = _pick(M, (2112, 1536, 768, 384, 128))
    tr = _pick(M, (176, 128))
    tt = _pick(Lp, (528, 384, 128))
    ta = _pick(Lp, tuple(c for c in (384, 128) if c + 128 <= Lp))

    meta = jnp.broadcast_to(meta_tokens[None].astype(x.dtype), (B, N_META, D))
    h = jnp.concatenate([meta, x, jnp.zeros((B, Lp - L, D), x.dtype)], axis=1).reshape(M, D)
    cos_t, sin_t = _rope_tables(Lp, B)

    f1_w13, f1_w2 = ffn1_w13, ffn1_w2.astype(BF16)
    f2_w13, f2_w2 = ffn2_w13, ffn2_w2.astype(BF16)
    w_small, w_gate, w_uq, w_ukv = _prep_mixer(w_in, mla_w_uq, mla_w_ukv)
    w_mla_o, w_glu, w_dn_o, w_mix = (t.astype(BF16) for t in (mla_w_o, s5_w_glu, dn_w_o, w_out))
    q_scale = MLA_QK ** -0.5 * math.log2(math.e)

    xn = rmsnorm_cast(h, sandwich_g[0, 0], col_block=0, width=D, tm=tr)
    for l in range(depth):
        g = sandwich_g[l]
        y = _ffn(xn, f1_w13, f1_w2, l, tm_up=tm_up, tm=tm)
        h, xn = resid_norm(h, y, g[1], g[2], coef=0.5, tm=tr)

        p = matmul(xn, w_small, l, tm=tm, tn=768, out_dtype=F32, w_transposed=True,
                   name="in_proj_small")
        gates = matmul(xn, w_gate, l, tm=tm, tn=1024, out_dtype=BF16, act="sigmoid",
                       w_transposed=True, name="in_proj_gates")

        qn = rmsnorm_cast(p, mla_q_norm_g[l], col_block=P_Q // MLA_Q_LORA, width=MLA_Q_LORA, tm=tm)
        kvn = rmsnorm_cast(p, mla_kv_norm_g[l], col_block=P_KV // MLA_KV_LORA, width=MLA_KV_LORA, tm=tm)
        q = matmul_qrope(qn, w_uq, l, cos_t, sin_t, tm=tm, heads_per_tile=4, scale=q_scale)
        kv = matmul(kvn, w_ukv, l, tm=tm, tn=1024, out_dtype=BF16, name="mla_kv_proj")
        kr = k_rope(p, cos_t, sin_t, tm=tm)
        o_mla = mla_attention(q, kv, kr, batch=B, seq_pad=Lp, tq=ta)

        s5c = s5_constants(s5_a_re[l], s5_a_im[l], s5_log_dt[l], s5_b_re[l], s5_b_im[l],
                           s5_c_re[l], s5_c_im[l], s5_d[l])
        h_s5 = s5_scan(p, s5c, batch=B, seq_pad=Lp, tt=tt)

        ab_t = jnp.transpose(p[:, P_AB:P_AB + 2 * DN_HEADS].reshape(B, Lp, 2 * DN_HEADS), (0, 2, 1))
        gcum, beta = dn_gates(ab_t, dn_a_log[l], dn_dt_bias[l])
        u, w, qg, kd, qk, dl = dn_prep(p, dn_conv_w[l], beta[..., None], gcum[..., None],
                                       gcum.reshape(B, DN_HEADS, Lp // CHUNK, CHUNK),
                                       batch=B, seq_pad=Lp)
        o_dn = dn_scan(u, w, qg, kd, qk, dl, p, dn_out_norm_g[l], batch=B, seq_pad=Lp)

        merged = merge_branches(o_mla, h_s5, o_dn, w_mla_o, w_glu, w_dn_o, gates, l, tm=tm, tn=512)
        mix = matmul(merged, w_mix, l, tm=tm, tn=1024, out_dtype=BF16, name="out_proj")
        h, xn = resid_norm(h, mix, g[3], g[4], coef=1.0, tm=tr)

        y = _ffn(xn, f2_w13, f2_w2, l, tm_up=tm_up, tm=tm)
        g_next = sandwich_g[l + 1, 0] if l + 1 < depth else None
        h, xn = resid_norm(h, y, g[5], g_next, coef=0.5, tm=tr)
    return h.reshape(B, Lp, D)[:, N_META:L]
```

```python
import functools
import math

import jax
import jax.numpy as jnp
from jax import lax
from jax.experimental import pallas as pl
from jax.experimental.pallas import tpu as pltpu

F32 = jnp.float32
BF16 = jnp.bfloat16

D_MODEL = 4096
CHUNK = 64
N_META = 16
EPS = 1e-6

MLA_HEADS = 16
MLA_Q_LORA = 1024
MLA_KV_LORA = 512
MLA_NOPE = 128
MLA_ROPE = 64
MLA_V = 128
MLA_QK = MLA_NOPE + MLA_ROPE
MLA_WIDTH = MLA_HEADS * MLA_V
MLA_QPAD = 256
ROPE_THETA = 10000.0

S5_WIDTH = 1024
S5_GROUP = 16
S5_GROUPS = S5_WIDTH // S5_GROUP
S5_STATE = 64
S5_GB = 4
S5_GPB = S5_GROUPS // S5_GB

DN_HEADS = 8
DN_DK = 128
DN_DV = 128
DN_QK = DN_HEADS * DN_DK
DN_WIDTH = DN_HEADS * DN_DV
DN_CONV = 4

FFN_HIDDEN = 11008

OFF_Q = 0
OFF_KV = OFF_Q + MLA_Q_LORA
OFF_KR = OFF_KV + MLA_KV_LORA
OFF_S5 = OFF_KR + MLA_ROPE
OFF_DN_QKV = OFF_S5 + S5_WIDTH
OFF_DN_Z = OFF_DN_QKV + 2 * DN_QK + DN_WIDTH
OFF_DN_A = OFF_DN_Z + DN_WIDTH
OFF_DN_B = OFF_DN_A + DN_HEADS
OFF_GATE = OFF_DN_B + DN_HEADS

P_Q = 0
P_DNZ = P_Q + MLA_Q_LORA
P_S5 = P_DNZ + DN_WIDTH
P_DNQKV = P_S5 + S5_WIDTH
P_KV = P_DNQKV + 3 * DN_QK
P_KR = P_KV + MLA_KV_LORA
P_AB = P_KR + 128
P_WIDTH = P_AB + 128

VMEM_LIMIT_BYTES = 56 * 1024 * 1024


def _cparams(*sem):
    return pltpu.CompilerParams(dimension_semantics=sem, vmem_limit_bytes=VMEM_LIMIT_BYTES)


def _pick(n, candidates):
    for c in candidates:
        if n % c == 0:
            return c
    raise ValueError(f"no tile in {candidates} divides {n}")


def _rms(x, g):
    return x * lax.rsqrt(jnp.mean(x * x, axis=-1, keepdims=True) + EPS) * g


def _rmsnorm_kernel(x_ref, g_ref, o_ref):
    o_ref[...] = _rms(x_ref[...], g_ref[...]).astype(o_ref.dtype)


def rmsnorm_cast(x, g, *, col_block, width, tm):
    M = x.shape[0]
    return pl.pallas_call(
        _rmsnorm_kernel,
        grid=(M // tm,),
        in_specs=[pl.BlockSpec((tm, width), lambda i: (i, col_block)),
                  pl.BlockSpec((1, width), lambda i: (0, 0))],
        out_specs=pl.BlockSpec((tm, width), lambda i: (i, 0)),
        out_shape=jax.ShapeDtypeStruct((M, width), BF16),
        compiler_params=_cparams("parallel"),
        name="rmsnorm_cast",
    )(x, g.reshape(1, width).astype(F32))


def _resid_norm_kernel(h_ref, y_ref, gpost_ref, gpre_ref, ho_ref, xn_ref, *, coef):
    h = h_ref[...] + coef * _rms(y_ref[...].astype(F32), gpost_ref[...])
    ho_ref[...] = h
    xn_ref[...] = _rms(h, gpre_ref[...]).astype(xn_ref.dtype)


def _resid_kernel(h_ref, y_ref, gpost_ref, ho_ref, *, coef):
    ho_ref[...] = h_ref[...] + coef * _rms(y_ref[...].astype(F32), gpost_ref[...])


def resid_out(h, y, g_post, *, coef, batch, seq_pad, first, count, tm):
    D = h.shape[1]
    nt = count // tm

    def rows(b, t):
        return pl.multiple_of(b * seq_pad + first + t * tm, 16), 0

    row = pl.BlockSpec((pl.Element(tm), pl.Element(D)), rows)
    return pl.pallas_call(
        functools.partial(_resid_kernel, coef=coef),
        grid=(batch, nt),
        in_specs=[row, row, pl.BlockSpec((1, D), lambda b, t: (0, 0))],
        out_specs=pl.BlockSpec((tm, D), lambda b, t: (b * nt + t, 0)),
        out_shape=jax.ShapeDtypeStruct((batch * count, D), F32),
        compiler_params=_cparams("parallel", "parallel"), name="resid_out",
    )(h, y, g_post.reshape(1, D))


def resid_norm(h, y, g_post, g_pre, *, coef, tm):
    M, D = h.shape
    row = pl.BlockSpec((tm, D), lambda i: (i, 0))
    gain = pl.BlockSpec((1, D), lambda i: (0, 0))
    return pl.pallas_call(
        functools.partial(_resid_norm_kernel, coef=coef),
        grid=(M // tm,), in_specs=[row, row, gain, gain], out_specs=[row, row],
        out_shape=[jax.ShapeDtypeStruct((M, D), F32), jax.ShapeDtypeStruct((M, D), BF16)],
        input_output_aliases={0: 0},
        compiler_params=_cparams("parallel"), name="resid_norm",
    )(h, y, g_post.reshape(1, D), g_pre.reshape(1, D))


def _mm_kernel(x_ref, w_ref, o_ref, *, act, w_transposed):
    if w_transposed:
        acc = lax.dot_general(x_ref[...], w_ref[0], (((1,), (1,)), ((), ())),
                              preferred_element_type=F32)
    else:
        acc = jnp.dot(x_ref[...], w_ref[...], preferred_element_type=F32)
    if act == "sigmoid":
        acc = jax.nn.sigmoid(acc)
    o_ref[...] = acc.astype(o_ref.dtype)


def matmul(x, w, layer, *, tm, tn, out_dtype, act=None, w_rows=None, name="matmul"):
    M, K = x.shape
    if w_rows is None:
        N = w.shape[2]
        w_spec = pl.BlockSpec((None, K, tn), lambda i, j: (layer, 0, j))
    else:
        first, N = w_rows
        w_spec = pl.BlockSpec((pl.Element(1), pl.Element(tn), pl.Element(K)),
                              lambda i, j: (layer, pl.multiple_of(first + j * tn, 16), 0))
    return pl.pallas_call(
        functools.partial(_mm_kernel, act=act, w_transposed=w_rows is not None),
        grid=(M // tm, N // tn),
        in_specs=[pl.BlockSpec((tm, K), lambda i, j: (i, 0)), w_spec],
        out_specs=pl.BlockSpec((tm, tn), lambda i, j: (i, j)),
        out_shape=jax.ShapeDtypeStruct((M, N), out_dtype),
        compiler_params=_cparams("parallel", "parallel"),
        name=name,
    )(x, w)


def _mm_swiglu_kernel(x_ref, wa_ref, wb_ref, o_ref):
    x = x_ref[...]
    a = jnp.dot(x, wa_ref[...].astype(BF16), preferred_element_type=F32)
    b = jnp.dot(x, wb_ref[...].astype(BF16), preferred_element_type=F32)
    o_ref[...] = (jax.nn.silu(a) * b).astype(o_ref.dtype)


def matmul_swiglu(x, w13, layer, *, tm, tn):
    M, K = x.shape
    N = w13.shape[2] // 2
    nj = N // tn
    return pl.pallas_call(
        _mm_swiglu_kernel,
        grid=(M // tm, nj),
        in_specs=[pl.BlockSpec((tm, K), lambda i, j: (i, 0), pipeline_mode=pl.Buffered(1)),
                  pl.BlockSpec((None, K, tn), lambda i, j: (layer, 0, j)),
                  pl.BlockSpec((None, K, tn), lambda i, j: (layer, 0, nj + j))],
        out_specs=pl.BlockSpec((tm, tn), lambda i, j: (i, j)),
        out_shape=jax.ShapeDtypeStruct((M, N), BF16),
        compiler_params=_cparams("parallel", "parallel"),
        name="ffn_up_swiglu",
    )(x, w13, w13)


def _rope_block(a, c, s):
    return a * c + pltpu.roll(a, 64, axis=1) * s


def _mm_qrope_kernel(x_ref, w_ref, c_ref, s_ref, o_ref, *, heads, scale):
    acc = jnp.dot(x_ref[...], w_ref[...], preferred_element_type=F32) * scale
    c = c_ref[...]
    s = s_ref[...]
    for hb in range(heads):
        lo = hb * MLA_QPAD
        o_ref[:, lo:lo + 128] = acc[:, lo:lo + 128].astype(o_ref.dtype)
        o_ref[:, lo + 128:lo + 256] = _rope_block(acc[:, lo + 128:lo + 256], c, s).astype(o_ref.dtype)


def matmul_qrope(x, w, layer, cos_t, sin_t, *, tm, heads_per_tile, scale):
    M, K = x.shape
    N = w.shape[2]
    tn = heads_per_tile * MLA_QPAD
    tab = pl.BlockSpec((tm, 128), lambda i, j: (i, 0))
    return pl.pallas_call(
        functools.partial(_mm_qrope_kernel, heads=heads_per_tile, scale=scale),
        grid=(M // tm, N // tn),
        in_specs=[pl.BlockSpec((tm, K), lambda i, j: (i, 0)),
                  pl.BlockSpec((None, K, tn), lambda i, j: (layer, 0, j)), tab, tab],
        out_specs=pl.BlockSpec((tm, tn), lambda i, j: (i, j)),
        out_shape=jax.ShapeDtypeStruct((M, N), BF16),
        compiler_params=_cparams("parallel", "parallel"),
        name="mla_q_proj_rope",
    )(x, w, cos_t, sin_t)


def _krope_kernel(x_ref, c_ref, s_ref, o_ref):
    o_ref[...] = _rope_block(x_ref[...], c_ref[...], s_ref[...]).astype(o_ref.dtype)


def k_rope(p, cos_t, sin_t, *, tm):
    M = p.shape[0]
    tab = pl.BlockSpec((tm, 128), lambda i: (i, 0))
    return pl.pallas_call(
        _krope_kernel,
        grid=(M // tm,),
        in_specs=[pl.BlockSpec((tm, 128), lambda i: (i, P_KR // 128)), tab, tab],
        out_specs=tab,
        out_shape=jax.ShapeDtypeStruct((M, 128), BF16),
        compiler_params=_cparams("parallel"),
        name="mla_k_rope",
    )(p, cos_t, sin_t)


def _merge_kernel(o_ref, hs_ref, dn_ref, wo_ref, wv_ref, wg_ref, wd_ref,
                  g0_ref, g1_ref, g2_ref, out_ref):
    hs = hs_ref[...]
    y_mla = jnp.dot(o_ref[...], wo_ref[...], preferred_element_type=F32)
    val = jnp.dot(hs, wv_ref[...], preferred_element_type=F32)
    gate = jnp.dot(hs, wg_ref[...], preferred_element_type=F32)
    y_dn = jnp.dot(dn_ref[...], wd_ref[...], preferred_element_type=F32)
    merged = (g0_ref[...].astype(F32) * y_mla
              + g1_ref[...].astype(F32) * (val * jax.nn.sigmoid(gate))
              + g2_ref[...].astype(F32) * y_dn)
    out_ref[...] = merged.astype(out_ref.dtype)


def merge_branches(o_mla, h_s5, o_dn, w_o, w_glu, w_dn_o, gates, layer, *, tm, tn):
    M = o_mla.shape[0]
    D = w_o.shape[2]
    nj = D // tn

    def rows(width):
        return pl.BlockSpec((tm, width), lambda i, j: (i, 0))

    def cols(kdim, off):
        return pl.BlockSpec((None, kdim, tn), lambda i, j: (layer, 0, off * nj + j))

    def gate(b):
        return pl.BlockSpec((tm, tn), lambda i, j: (i, b * nj + j))

    return pl.pallas_call(
        _merge_kernel,
        grid=(M // tm, nj),
        in_specs=[rows(MLA_WIDTH), rows(S5_WIDTH), rows(DN_WIDTH),
                  cols(MLA_WIDTH, 0), cols(S5_WIDTH, 0), cols(S5_WIDTH, 1), cols(DN_WIDTH, 0),
                  gate(0), gate(1), gate(2)],
        out_specs=pl.BlockSpec((tm, tn), lambda i, j: (i, j)),
        out_shape=jax.ShapeDtypeStruct((M, D), BF16),
        compiler_params=_cparams("parallel", "parallel"),
        name="merge_branches",
    )(o_mla, h_s5, o_dn, w_o, w_glu, w_glu, w_dn_o, gates, gates, gates)


_NEG = -1e30


def _chunk_id(pos):
    return (pos + (CHUNK - N_META)) >> 6


ATTN_HEADS_PER_STEP = 4


def _attn_kernel(q_ref, kn_ref, kr_ref, v_ref, o_ref, *, tq, seq_pad):
    G = ATTN_HEADS_PER_STEP
    tk = tq
    tkm = tq + 128
    q0 = pl.program_id(2) * tq
    qs = [q_ref[:, g * MLA_QPAD:(g + 1) * MLA_QPAD] for g in range(G)]
    full_end = (_chunk_id(q0) + 1) * CHUNK - (CHUNK - N_META)
    n_full = full_end // tk

    def step(ks, width, carry, mask):
        ms, accs = carry
        k_rope = kr_ref[pl.ds(ks, width), :]
        ones = jnp.ones((width, 128), BF16)

        def scores(g):
            k = jnp.concatenate([kn_ref[pl.ds(ks, width), g * 128:(g + 1) * 128], k_rope], axis=1)
            s = lax.dot_general(qs[g], k, (((1,), (1,)), ((), ())), preferred_element_type=F32)
            return s if mask is None else jnp.where(mask, s, _NEG)

        def weights(g, s):
            m_new = jnp.maximum(ms[g], jnp.max(s, axis=-1, keepdims=True))
            return m_new, jnp.exp2(s - m_new).astype(BF16), jnp.exp2(ms[g] - m_new)

        def update(g, p, alpha):
            v1 = jnp.concatenate([v_ref[pl.ds(ks, width), g * 128:(g + 1) * 128], ones], axis=1)
            return alpha * accs[g] + jnp.dot(p, v1, preferred_element_type=F32)

        ss = [scores(g) for g in range(G)]
        m_new, ps, alphas = zip(*[weights(g, ss[g]) for g in range(G)])
        return list(m_new), [update(g, ps[g], alphas[g]) for g in range(G)]

    def full_steps(lo, hi, width, carry):
        return lax.fori_loop(
            lo, hi, lambda j, c: step(pl.multiple_of(j * width, width), width, c, None), carry)

    init = ([jnp.full((tq, 1), _NEG, F32)] * G, [jnp.zeros((tq, 2 * MLA_V), F32)] * G)
    carry = full_steps(0, n_full // 4, 4 * tk, init)
    carry = full_steps((n_full // 4) * 2, n_full // 2, 2 * tk, carry)
    carry = full_steps(n_full - n_full % 2, n_full, tk, carry)

    first = n_full * tk
    ks = pl.multiple_of(jnp.minimum(first, seq_pad - tkm), 128)
    kpos = ks + lax.broadcasted_iota(jnp.int32, (1, tkm), 1)
    cq = _chunk_id(q0 + lax.broadcasted_iota(jnp.int32, (tq, 1), 0))
    mask = jnp.logical_and(cq >= _chunk_id(kpos), kpos >= first)
    _, accs = step(ks, tkm, carry, mask)
    for g in range(G):
        o_ref[:, g * MLA_V:(g + 1) * MLA_V] = (
            accs[g][:, :MLA_V] / accs[g][:, MLA_V:]).astype(o_ref.dtype)


def mla_attention(q, kv, kr, *, batch, seq_pad, tq):
    M = q.shape[0]
    nq = seq_pad // tq
    G = ATTN_HEADS_PER_STEP
    ng = MLA_HEADS // G
    assert tq + 128 <= seq_pad and tq > CHUNK
    return pl.pallas_call(
        functools.partial(_attn_kernel, tq=tq, seq_pad=seq_pad),
        grid=(batch, ng, nq),
        in_specs=[pl.BlockSpec((tq, G * MLA_QPAD), lambda b, h, i: (b * nq + i, h)),
                  pl.BlockSpec((seq_pad, G * 128), lambda b, h, i: (b, h)),
                  pl.BlockSpec((seq_pad, 128), lambda b, h, i: (b, 0)),
                  pl.BlockSpec((seq_pad, G * 128), lambda b, h, i: (b, ng + h))],
        out_specs=pl.BlockSpec((tq, G * MLA_V), lambda b, h, i: (b * nq + i, h)),
        out_shape=jax.ShapeDtypeStruct((M, MLA_WIDTH), BF16),
        compiler_params=_cparams("parallel", "parallel", "parallel"),
        name="mla_attention",
    )(q, kv, kr, kv)


def _s5_kernel(u_ref, br_ref, bi_ref, cr_ref, ci_ref, d_ref, ast_ref, pw_ref, o_ref,
               xr_s, xi_s, car_s, *, tt):
    n = S5_GPB * S5_STATE

    @pl.when(pl.program_id(2) == 0)
    def _():
        car_s[...] = jnp.zeros_like(car_s)

    u = u_ref[...]
    ub = u.astype(BF16)
    xr = jnp.dot(ub, br_ref[...], preferred_element_type=F32).reshape(tt // 8, 8, n)
    xi = jnp.dot(ub, bi_ref[...], preferred_element_type=F32).reshape(tt // 8, 8, n)
    for si in range(3):
        ar = ast_ref[2 * si]
        ai = ast_ref[2 * si + 1]
        sr = pltpu.roll(xr, 1 << si, axis=1)
        sm = pltpu.roll(xi, 1 << si, axis=1)
        xr, xi = xr + (ar * sr - ai * sm), xi + (ar * sm + ai * sr)
    xr_s[...] = xr.reshape(tt, n)
    xi_s[...] = xi.reshape(tt, n)
    pr = pw_ref[0]
    pi = pw_ref[1]

    def body(i, carry):
        cr, ci = carry
        sl = pl.ds(pl.multiple_of(i * 8, 8), 8)
        a = xr_s[sl, :] + (pr * cr - pi * ci)
        b = xi_s[sl, :] + (pr * ci + pi * cr)
        xr_s[sl, :] = a
        xi_s[sl, :] = b
        return (jnp.broadcast_to(a[7:8, :], (8, n)), jnp.broadcast_to(b[7:8, :], (8, n)))

    cr, ci = lax.fori_loop(0, tt // 8, body, (car_s[0], car_s[1]))
    car_s[0] = cr
    car_s[1] = ci
    y = (jnp.dot(xr_s[...].astype(BF16), cr_ref[...], preferred_element_type=F32)
         - jnp.dot(xi_s[...].astype(BF16), ci_ref[...], preferred_element_type=F32)
         + d_ref[...] * u)
    o_ref[...] = jax.nn.gelu(y).astype(o_ref.dtype)


def s5_scan(p, s5c, *, batch, seq_pad, tt):
    M = p.shape[0]
    nt = seq_pad // tt
    cw = S5_GPB * S5_GROUP
    n = S5_GPB * S5_STATE
    ublk = P_S5 // cw
    return pl.pallas_call(
        functools.partial(_s5_kernel, tt=tt),
        grid=(batch, S5_GB, nt),
        in_specs=[pl.BlockSpec((tt, cw), lambda b, g, t: (b * nt + t, ublk + g)),
                  pl.BlockSpec((None, cw, n), lambda b, g, t: (g, 0, 0)),
                  pl.BlockSpec((None, cw, n), lambda b, g, t: (g, 0, 0)),
                  pl.BlockSpec((None, n, cw), lambda b, g, t: (g, 0, 0)),
                  pl.BlockSpec((None, n, cw), lambda b, g, t: (g, 0, 0)),
                  pl.BlockSpec((None, 1, cw), lambda b, g, t: (g, 0, 0)),
                  pl.BlockSpec((None, 6, 8, n), lambda b, g, t: (g, 0, 0, 0)),
                  pl.BlockSpec((None, 2, 8, n), lambda b, g, t: (g, 0, 0, 0))],
        out_specs=pl.BlockSpec((tt, cw), lambda b, g, t: (b * nt + t, g)),
        out_shape=jax.ShapeDtypeStruct((M, S5_WIDTH), BF16),
        scratch_shapes=[pltpu.VMEM((tt, n), F32), pltpu.VMEM((tt, n), F32),
                        pltpu.VMEM((2, 8, n), F32)],
        compiler_params=_cparams("parallel", "parallel", "arbitrary"),
        name="s5_scan",
    )(p, s5c["br"], s5c["bi"], s5c["cr"], s5c["ci"], s5c["d"], s5c["ast"], s5c["pw"])


def s5_constants(a_re, a_im, log_dt, b_re, b_im, c_re, c_im, d):
    ar, ai = a_re.astype(F32), a_im.astype(F32)
    delta = jnp.exp(log_dt.astype(F32))[:, None]
    mag = jnp.exp(ar * delta)
    abar_r, abar_i = mag * jnp.cos(ai * delta), mag * jnp.sin(ai * delta)
    den = ar * ar + ai * ai
    zr = ((abar_r - 1.0) * ar + abar_i * ai) / den
    zi = (abar_i * ar - (abar_r - 1.0) * ai) / den
    br, bi = b_re.astype(F32), b_im.astype(F32)
    bbar_r = zr[..., None] * br - zi[..., None] * bi
    bbar_i = zr[..., None] * bi + zi[..., None] * br

    eye = jnp.eye(S5_GPB, dtype=F32)

    def in_blockdiag(m):
        m = m.reshape(S5_GB, S5_GPB, S5_STATE, S5_GROUP)
        bd = jnp.einsum("bgpc,gh->bgchp", m, eye)
        return bd.reshape(S5_GB, S5_GPB * S5_GROUP, S5_GPB * S5_STATE).astype(BF16)

    def out_blockdiag(m):
        m = m.reshape(S5_GB, S5_GPB, S5_GROUP, S5_STATE)
        bd = jnp.einsum("bgcp,gh->bgphc", m, eye)
        return bd.reshape(S5_GB, S5_GPB * S5_STATE, S5_GPB * S5_GROUP).astype(BF16)

    def cmul(x, y):
        return x[0] * y[0] - x[1] * y[1], x[0] * y[1] + x[1] * y[0]

    n = S5_GPB * S5_STATE
    a1 = (abar_r.reshape(S5_GB, n), abar_i.reshape(S5_GB, n))
    a2 = cmul(a1, a1)
    a4 = cmul(a2, a2)
    row = jnp.arange(8)[None, :, None]
    ast = []
    for s, a in ((1, a1), (2, a2), (4, a4)):
        for part in a:
            ast.append(jnp.where(row >= s, part[:, None, :], 0.0))
    ast = jnp.stack(ast, axis=1)
    pows = [a1]
    for _ in range(7):
        pows.append(cmul(pows[-1], a1))
    pw = jnp.stack([jnp.stack([p[0] for p in pows], axis=1),
                    jnp.stack([p[1] for p in pows], axis=1)], axis=1)
    return {
        "br": in_blockdiag(bbar_r), "bi": in_blockdiag(bbar_i),
        "cr": out_blockdiag(c_re.astype(F32)), "ci": out_blockdiag(c_im.astype(F32)),
        "d": d.astype(F32).reshape(S5_GB, 1, S5_GPB * S5_GROUP),
        "ast": ast, "pw": pw,
    }


def _dn_gates_kernel(ab_ref, alog_ref, dtb_ref, gcum_ref, beta_ref):
    ab = ab_ref[...]
    g = -jnp.exp(alog_ref[...]) * jax.nn.softplus(ab[:DN_HEADS] + dtb_ref[...])
    beta_ref[...] = jax.nn.sigmoid(ab[DN_HEADS:])
    lane = lax.broadcasted_iota(jnp.int32, g.shape, 1) % CHUNK
    s = 1
    while s < CHUNK:
        g = g + jnp.where(lane >= s, pltpu.roll(g, s, axis=1), 0.0)
        s *= 2
    gcum_ref[...] = g


def dn_gates(ab_t, a_log, dt_bias):
    B, _, Lp = ab_t.shape
    col = pl.BlockSpec((DN_HEADS, 1), lambda b: (0, 0))
    out = pl.BlockSpec((None, DN_HEADS, Lp), lambda b: (b, 0, 0))
    return pl.pallas_call(
        _dn_gates_kernel,
        grid=(B,),
        in_specs=[pl.BlockSpec((None, 2 * DN_HEADS, Lp), lambda b: (b, 0, 0)), col, col],
        out_specs=[out, out],
        out_shape=[jax.ShapeDtypeStruct((B, DN_HEADS, Lp), F32)] * 2,
        compiler_params=_cparams("parallel"),
        name="dn_gates",
    )(ab_t, a_log.astype(F32).reshape(DN_HEADS, 1), dt_bias.astype(F32).reshape(DN_HEADS, 1))


def _dot(a, b):
    return jnp.dot(a, b, preferred_element_type=F32)


def _dot_nt(a, b):
    return lax.dot_general(a, b, (((1,), (1,)), ((), ())), preferred_element_type=F32)


def _dot_tn(a, b):
    return lax.dot_general(a, b, (((0,), (0,)), ((), ())), preferred_element_type=F32)


def _l2(x):
    return x * lax.rsqrt(jnp.sum(x * x, axis=-1, keepdims=True) + EPS)


def _dn_prep_kernel(q_ref, k_ref, v_ref, cwq_ref, cwk_ref, cwv_ref, bcol_ref, gcol_ref, grow_ref,
                    u_ref, w_ref, qg_ref, kd_ref, qk_ref, dl_ref, *, n_chunks, unroll):
    C = CHUNK

    def conv_silu(x_ref, cw_ref, c, sl):
        cur = x_ref[sl, :]
        halo = x_ref[pl.ds(pl.multiple_of(jnp.maximum(c * C - 8, 0), 8), 8), :]
        xc = jnp.concatenate([jnp.where(c > 0, halo, 0.0), cur], axis=0)
        acc = cur * cw_ref[DN_CONV - 1:DN_CONV, :]
        for j in range(DN_CONV - 1):
            lo = 8 - (DN_CONV - 1) + j
            acc = acc + xc[lo:lo + C, :] * cw_ref[j:j + 1, :]
        return jax.nn.silu(acc)

    ii = lax.broadcasted_iota(jnp.int32, (C, C), 0)
    jj = lax.broadcasted_iota(jnp.int32, (C, C), 1)
    tri = ii >= jj
    strict = ii > jj
    eye = (ii == jj).astype(F32)

    def body(i, carry):
        U = range(unroll)
        cs = [i * unroll + j for j in U]
        sls = [pl.ds(pl.multiple_of(c * C, C), C) for c in cs]
        q = [_l2(conv_silu(q_ref, cwq_ref, cs[j], sls[j])) * (DN_DK ** -0.5) for j in U]
        k = [_l2(conv_silu(k_ref, cwk_ref, cs[j], sls[j])) for j in U]
        v = [conv_silu(v_ref, cwv_ref, cs[j], sls[j]) for j in U]
        beta = [bcol_ref[sl, :] for sl in sls]
        gc = [gcol_ref[sl, :] for sl in sls]
        decay = [jnp.exp(jnp.where(tri, gc[j] - grow_ref[pl.ds(cs[j], 1), :], -jnp.inf)) for j in U]
        kb = [k[j] * beta[j] for j in U]
        a_mat = [jnp.where(strict, _dot_nt(kb[j], k[j]) * decay[j], 0.0) for j in U]
        t_inv = [eye - a for a in a_mat]
        pw = a_mat
        for _ in range(5):
            pw = [_dot(x, x) for x in pw]
            t_inv = [t + _dot(t, x) for t, x in zip(t_inv, pw)]
        u = [_dot(t_inv[j], v[j] * beta[j]) for j in U]
        w = [_dot(t_inv[j], kb[j] * jnp.exp(gc[j])) for j in U]
        qk = [_dot_nt(q[j].astype(BF16), k[j].astype(BF16)) * decay[j] for j in U]
        for j in U:
            sl = sls[j]
            u_ref[sl, :] = u[j]
            w_ref[sl, :] = w[j].astype(w_ref.dtype)
            qk_ref[sl, :] = qk[j].astype(qk_ref.dtype)
            qg_ref[sl, :] = (q[j] * jnp.exp(gc[j])).astype(qg_ref.dtype)
            g_last = gc[j][C - 1:C, :]
            kd_ref[sl, :] = (k[j] * jnp.exp(g_last - gc[j])).astype(kd_ref.dtype)
            dl_ref[pl.ds(cs[j], 1), :] = jnp.broadcast_to(jnp.exp(g_last), (1, dl_ref.shape[1]))
        return carry

    lax.fori_loop(0, n_chunks // unroll, body, 0)


def dn_prep(p, conv_w, beta_col, gcum_col, gcum_row, *, batch, seq_pad):
    M = p.shape[0]
    H = DN_HEADS
    nc = seq_pad // CHUNK
    c0 = P_DNQKV // 128

    def head(off):
        return pl.BlockSpec((seq_pad, 128), lambda b, h: (b, off + h))

    def taps(off):
        return pl.BlockSpec((DN_CONV, 128), lambda b, h: (0, off + h))

    colspec = pl.BlockSpec((None, None, seq_pad, 1), lambda b, h: (b, h, 0, 0))
    wide = jax.ShapeDtypeStruct((M, DN_WIDTH), BF16)
    conv_w = conv_w.astype(F32)
    return pl.pallas_call(
        functools.partial(_dn_prep_kernel, n_chunks=nc, unroll=_pick(nc, (11, 6, 4, 3, 2, 1))),
        grid=(batch, H),
        in_specs=[head(c0), head(c0 + H), head(c0 + 2 * H), taps(0), taps(H), taps(2 * H),
                  colspec, colspec,
                  pl.BlockSpec((None, None, nc, CHUNK), lambda b, h: (b, h, 0, 0))],
        out_specs=[head(0), head(0), head(0), head(0),
                   pl.BlockSpec((None, None, seq_pad, CHUNK), lambda b, h: (b, h, 0, 0)),
                   pl.BlockSpec((None, None, nc, 128), lambda b, h: (b, h, 0, 0))],
        out_shape=[jax.ShapeDtypeStruct((M, DN_WIDTH), F32), wide, wide, wide,
                   jax.ShapeDtypeStruct((batch, H, seq_pad, CHUNK), BF16),
                   jax.ShapeDtypeStruct((batch, H, nc, 128), F32)],
        compiler_params=_cparams("parallel", "parallel"),
        name="dn_prep",
    )(p, p, p, conv_w, conv_w, conv_w, beta_col, gcum_col, gcum_row)


def _dn_scan_kernel(u_ref, w_ref, qg_ref, kd_ref, qk_ref, dl_ref, z_ref, gn_ref, o_ref, s_ref,
                    *, chunks_per_tile):
    C = CHUNK

    @pl.when(pl.program_id(1) == 0)
    def _():
        s_ref[...] = jnp.zeros_like(s_ref)

    gn = gn_ref[...]

    def body(c, carry):
        sl = pl.ds(pl.multiple_of(c * C, C), C)
        H = range(DN_HEADS)
        hs = [slice(j * DN_DV, (j + 1) * DN_DV) for j in H]
        S = [s_ref[j] for j in H]
        Sb = [s.astype(BF16) for s in S]
        v_new = [u_ref[sl, hs[j]] - _dot(w_ref[sl, hs[j]], Sb[j]) for j in H]
        o_inter = [_dot(qg_ref[sl, hs[j]], Sb[j]) for j in H]
        vb = [v.astype(BF16) for v in v_new]
        o = [o_inter[j] + _dot(qk_ref[j, sl, :], vb[j]) for j in H]
        s_upd = [_dot_tn(kd_ref[sl, hs[j]], vb[j]) for j in H]
        for j in H:
            s_ref[j] = S[j] * dl_ref[j, pl.ds(c, 1), :] + s_upd[j]
            o_ref[sl, hs[j]] = (_rms(o[j], gn) * jax.nn.silu(z_ref[sl, hs[j]])).astype(o_ref.dtype)
        return carry

    lax.fori_loop(0, chunks_per_tile, body, 0)


def dn_scan(u, w, qg, kd, qk, dl, p, out_norm_g, *, batch, seq_pad):
    M = p.shape[0]
    H = DN_HEADS
    nc = seq_pad // CHUNK
    cpt = _pick(nc, (11, 6, 4, 3, 2, 1))
    nt = nc // cpt
    tt = cpt * CHUNK
    dl = dl.reshape(batch, H, nt, cpt, 128)

    def rows(col_block=0):
        return pl.BlockSpec((tt, DN_WIDTH), lambda b, t: (b * nt + t, col_block))

    return pl.pallas_call(
        functools.partial(_dn_scan_kernel, chunks_per_tile=cpt),
        grid=(batch, nt),
        in_specs=[rows(), rows(), rows(), rows(),
                  pl.BlockSpec((None, H, tt, CHUNK), lambda b, t: (b, 0, t, 0)),
                  pl.BlockSpec((None, H, None, cpt, 128), lambda b, t: (b, 0, t, 0, 0)),
                  rows(P_DNZ // DN_WIDTH),
                  pl.BlockSpec((1, DN_DV), lambda b, t: (0, 0))],
        out_specs=rows(),
        out_shape=jax.ShapeDtypeStruct((M, DN_WIDTH), BF16),
        scratch_shapes=[pltpu.VMEM((H, DN_DK, DN_DV), F32)],
        compiler_params=_cparams("parallel", "arbitrary"),
        name="dn_scan",
    )(u, w, qg, kd, qk, dl, p, out_norm_g.astype(F32).reshape(1, DN_DV))


def _rot_cols(w):
    half = w.shape[-1] // 2
    return jnp.concatenate([-w[..., half:], w[..., :half]], axis=-1)


def _prep_mixer(w_in, mla_w_uq, mla_w_ukv):
    depth = w_in.shape[0]
    wt = jnp.swapaxes(w_in, 1, 2).astype(BF16)
    kr = wt[:, OFF_KR:OFF_S5]
    kr_rot = jnp.concatenate([-kr[:, MLA_ROPE // 2:], kr[:, :MLA_ROPE // 2]], axis=1)
    w_small = jnp.concatenate([
        wt[:, OFF_Q:OFF_KV], wt[:, OFF_DN_Z:OFF_DN_A], wt[:, OFF_S5:OFF_DN_Z],
        wt[:, OFF_KV:OFF_KR], kr, kr_rot, wt[:, OFF_DN_A:OFF_GATE],
        jnp.zeros((depth, 128 - 2 * DN_HEADS, D_MODEL), BF16)], axis=1)
    uq = mla_w_uq.reshape(depth, MLA_Q_LORA, MLA_HEADS, MLA_QK)
    w_uq = jnp.concatenate([uq, _rot_cols(uq[..., MLA_NOPE:])], axis=-1)
    w_uq = w_uq.reshape(depth, MLA_Q_LORA, MLA_HEADS * MLA_QPAD).astype(BF16)
    ukv = mla_w_ukv.reshape(depth, MLA_KV_LORA, MLA_HEADS, 2, 128)
    w_ukv = jnp.transpose(ukv, (0, 1, 3, 2, 4)).reshape(depth, MLA_KV_LORA, 2 * MLA_HEADS * 128)
    return wt, w_small, w_uq, w_ukv.astype(BF16)


def _rope_tables(seq_pad, batch):
    inv = ROPE_THETA ** (-jnp.arange(0, MLA_ROPE, 2, dtype=F32) / MLA_ROPE)
    ang = jnp.arange(seq_pad, dtype=F32)[:, None] * inv[None, :]
    zeros = jnp.zeros((seq_pad, 128 - MLA_ROPE), F32)
    cos_t = jnp.concatenate([jnp.cos(ang), jnp.cos(ang), zeros], axis=1)
    sin_t = jnp.concatenate([jnp.sin(ang), jnp.sin(ang), zeros], axis=1)
    return jnp.tile(cos_t, (batch, 1)), jnp.tile(sin_t, (batch, 1))


def _ffn(xn, w13, w2, layer, *, tm_up, tm):
    hid = matmul_swiglu(xn, w13, layer, tm=tm_up, tn=256)
    return matmul(hid, w2, layer, tm=tm // 2, tn=512, out_dtype=BF16, name="ffn_down")


def kernel(x, meta_tokens, sandwich_g, ffn1_w13, ffn1_w2, w_in, mla_q_norm_g, mla_kv_norm_g, mla_w_uq, mla_w_ukv, mla_w_o, s5_a_re, s5_a_im, s5_log_dt, s5_b_re, s5_b_im, s5_c_re, s5_c_im, s5_d, s5_w_glu, dn_conv_w, dn_a_log, dn_dt_bias, dn_out_norm_g, dn_w_o, w_out, ffn2_w13, ffn2_w2):
    B, seq, D = x.shape
    depth = w_in.shape[0]
    L = N_META + seq
    assert (L - N_META) % CHUNK == 0, "pad keys are hidden by the chunk mask only on a chunk boundary"
    Lp = -(-L // 128) * 128
    M = B * Lp
    tm = _pick(M, (1056, 768, 384, 128))
    tm_up = _pick(M, (2112, 1536, 768, 384, 128))
    tr = _pick(M, (176, 128))
    tt = _pick(Lp, (528, 384, 128))
    ta = _pick(Lp, tuple(c for c in (384, 128) if c + 128 <= Lp))

    meta = jnp.broadcast_to(meta_tokens[None].astype(x.dtype), (B, N_META, D))
    h = jnp.concatenate([meta, x, jnp.zeros((B, Lp - L, D), x.dtype)], axis=1).reshape(M, D)
    cos_t, sin_t = _rope_tables(Lp, B)

    f1_w13, f1_w2 = ffn1_w13, ffn1_w2.astype(BF16)
    f2_w13, f2_w2 = ffn2_w13, ffn2_w2.astype(BF16)
    w_in_t, w_small, w_uq, w_ukv = _prep_mixer(w_in, mla_w_uq, mla_w_ukv)
    w_mla_o, w_glu, w_dn_o, w_mix = (t.astype(BF16) for t in (mla_w_o, s5_w_glu, dn_w_o, w_out))
    q_scale = MLA_QK ** -0.5 * math.log2(math.e)

    xn = rmsnorm_cast(h, sandwich_g[0, 0], col_block=0, width=D, tm=tr)
    for l in range(depth):
        g = sandwich_g[l]
        y = _ffn(xn, f1_w13, f1_w2, l, tm_up=tm_up, tm=tm)
        h, xn = resid_norm(h, y, g[1], g[2], coef=0.5, tm=tr)

        p = matmul(xn, w_small, l, tm=tm, tn=768, out_dtype=F32, w_rows=(0, P_WIDTH),
                   name="in_proj_small")
        gates = matmul(xn, w_in_t, l, tm=tm, tn=1024, out_dtype=BF16, act="sigmoid",
                       w_rows=(OFF_GATE, 3 * D), name="in_proj_gates")

        qn = rmsnorm_cast(p, mla_q_norm_g[l], col_block=P_Q // MLA_Q_LORA, width=MLA_Q_LORA, tm=tm)
        kvn = rmsnorm_cast(p, mla_kv_norm_g[l], col_block=P_KV // MLA_KV_LORA, width=MLA_KV_LORA, tm=tm)
        q = matmul_qrope(qn, w_uq, l, cos_t, sin_t, tm=tm, heads_per_tile=4, scale=q_scale)
        kv = matmul(kvn, w_ukv, l, tm=tm, tn=1024, out_dtype=BF16, name="mla_kv_proj")
        kr = k_rope(p, cos_t, sin_t, tm=tm)
        o_mla = mla_attention(q, kv, kr, batch=B, seq_pad=Lp, tq=ta)

        s5c = s5_constants(s5_a_re[l], s5_a_im[l], s5_log_dt[l], s5_b_re[l], s5_b_im[l],
                           s5_c_re[l], s5_c_im[l], s5_d[l])
        h_s5 = s5_scan(p, s5c, batch=B, seq_pad=Lp, tt=tt)

        ab_t = jnp.transpose(p[:, P_AB:P_AB + 2 * DN_HEADS].reshape(B, Lp, 2 * DN_HEADS), (0, 2, 1))
        gcum, beta = dn_gates(ab_t, dn_a_log[l], dn_dt_bias[l])
        u, w, qg, kd, qk, dl = dn_prep(p, dn_conv_w[l], beta[..., None], gcum[..., None],
                                       gcum.reshape(B, DN_HEADS, Lp // CHUNK, CHUNK),
                                       batch=B, seq_pad=Lp)
        o_dn = dn_scan(u, w, qg, kd, qk, dl, p, dn_out_norm_g[l], batch=B, seq_pad=Lp)

        merged = merge_branches(o_mla, h_s5, o_dn, w_mla_o, w_glu, w_dn_o, gates, l, tm=tm, tn=512)
        mix = matmul(merged, w_mix, l, tm=tm, tn=1024, out_dtype=BF16, name="out_proj")
        h, xn = resid_norm(h, mix, g[3], g[4], coef=1.0, tm=tr)

        y = _ffn(xn, f2_w13, f2_w2, l, tm_up=tm_up, tm=tm)
        if l + 1 < depth:
            h, xn = resid_norm(h, y, g[5], sandwich_g[l + 1, 0], coef=0.5, tm=tr)
    out = resid_out(h, y, g[5], coef=0.5, batch=B, seq_pad=Lp, first=N_META, count=seq,
                    tm=_pick(seq, (256, 128, 64)))
    return out.reshape(B, seq, D)
```

```python
import functools
import math

import jax
import jax.numpy as jnp
from jax import lax
from jax.experimental import pallas as pl
from jax.experimental.pallas import tpu as pltpu

F32 = jnp.float32
BF16 = jnp.bfloat16

D_MODEL = 4096
CHUNK = 64
N_META = 16
EPS = 1e-6

MLA_HEADS = 16
MLA_Q_LORA = 1024
MLA_KV_LORA = 512
MLA_NOPE = 128
MLA_ROPE = 64
MLA_V = 128
MLA_QK = MLA_NOPE + MLA_ROPE
MLA_WIDTH = MLA_HEADS * MLA_V
MLA_QPAD = 256
ROPE_THETA = 10000.0

S5_WIDTH = 1024
S5_GROUP = 16
S5_GROUPS = S5_WIDTH // S5_GROUP
S5_STATE = 64
S5_GB = 4
S5_GPB = S5_GROUPS // S5_GB

DN_HEADS = 8
DN_DK = 128
DN_DV = 128
DN_QK = DN_HEADS * DN_DK
DN_WIDTH = DN_HEADS * DN_DV
DN_CONV = 4

FFN_HIDDEN = 11008

OFF_Q = 0
OFF_KV = OFF_Q + MLA_Q_LORA
OFF_KR = OFF_KV + MLA_KV_LORA
OFF_S5 = OFF_KR + MLA_ROPE
OFF_DN_QKV = OFF_S5 + S5_WIDTH
OFF_DN_Z = OFF_DN_QKV + 2 * DN_QK + DN_WIDTH
OFF_DN_A = OFF_DN_Z + DN_WIDTH
OFF_DN_B = OFF_DN_A + DN_HEADS
OFF_GATE = OFF_DN_B + DN_HEADS

P_Q = 0
P_DNZ = P_Q + MLA_Q_LORA
P_S5 = P_DNZ + DN_WIDTH
P_DNQKV = P_S5 + S5_WIDTH
P_KV = P_DNQKV + 3 * DN_QK
P_KR = P_KV + MLA_KV_LORA
P_AB = P_KR + 128
P_WIDTH = P_AB + 128

VMEM_LIMIT_BYTES = 56 * 1024 * 1024


def _cparams(*sem):
    return pltpu.CompilerParams(dimension_semantics=sem, vmem_limit_bytes=VMEM_LIMIT_BYTES)


def _pick(n, candidates):
    for c in candidates:
        if n % c == 0:
            return c
    raise ValueError(f"no tile in {candidates} divides {n}")


def _rms(x, g):
    return x * lax.rsqrt(jnp.mean(x * x, axis=-1, keepdims=True) + EPS) * g


def _rmsnorm_kernel(x_ref, g_ref, o_ref):
    o_ref[...] = _rms(x_ref[...], g_ref[...]).astype(o_ref.dtype)


def rmsnorm_cast(x, g, *, col_block, width, tm):
    M = x.shape[0]
    return pl.pallas_call(
        _rmsnorm_kernel,
        grid=(M // tm,),
        in_specs=[pl.BlockSpec((tm, width), lambda i: (i, col_block)),
                  pl.BlockSpec((1, width), lambda i: (0, 0))],
        out_specs=pl.BlockSpec((tm, width), lambda i: (i, 0)),
        out_shape=jax.ShapeDtypeStruct((M, width), BF16),
        compiler_params=_cparams("parallel"),
        name="rmsnorm_cast",
    )(x, g.reshape(1, width).astype(F32))


def _resid_norm_kernel(h_ref, y_ref, gpost_ref, gpre_ref, ho_ref, xn_ref, *, coef):
    h = h_ref[...] + coef * _rms(y_ref[...].astype(F32), gpost_ref[...])
    ho_ref[...] = h
    xn_ref[...] = _rms(h, gpre_ref[...]).astype(xn_ref.dtype)


def _resid_kernel(h_ref, y_ref, gpost_ref, ho_ref, *, coef):
    ho_ref[...] = h_ref[...] + coef * _rms(y_ref[...].astype(F32), gpost_ref[...])


def resid_out(h, y, g_post, *, coef, batch, seq_pad, first, count, tm):
    D = h.shape[1]
    nt = count // tm

    def rows(b, t):
        return pl.multiple_of(b * seq_pad + first + t * tm, 16), 0

    row = pl.BlockSpec((pl.Element(tm), pl.Element(D)), rows)
    return pl.pallas_call(
        functools.partial(_resid_kernel, coef=coef),
        grid=(batch, nt),
        in_specs=[row, row, pl.BlockSpec((1, D), lambda b, t: (0, 0))],
        out_specs=pl.BlockSpec((tm, D), lambda b, t: (b * nt + t, 0)),
        out_shape=jax.ShapeDtypeStruct((batch * count, D), F32),
        compiler_params=_cparams("parallel", "parallel"), name="resid_out",
    )(h, y, g_post.reshape(1, D))


def resid_norm(h, y, g_post, g_pre, *, coef, tm):
    M, D = h.shape
    row = pl.BlockSpec((tm, D), lambda i: (i, 0))
    gain = pl.BlockSpec((1, D), lambda i: (0, 0))
    return pl.pallas_call(
        functools.partial(_resid_norm_kernel, coef=coef),
        grid=(M // tm,), in_specs=[row, row, gain, gain], out_specs=[row, row],
        out_shape=[jax.ShapeDtypeStruct((M, D), F32), jax.ShapeDtypeStruct((M, D), BF16)],
        input_output_aliases={0: 0},
        compiler_params=_cparams("parallel"), name="resid_norm",
    )(h, y, g_post.reshape(1, D), g_pre.reshape(1, D))


def _mm_kernel(x_ref, w_ref, o_ref, *, act, w_transposed):
    if w_transposed:
        acc = lax.dot_general(x_ref[...], w_ref[...], (((1,), (1,)), ((), ())),
                              preferred_element_type=F32)
    else:
        acc = jnp.dot(x_ref[...], w_ref[...], preferred_element_type=F32)
    if act == "sigmoid":
        acc = jax.nn.sigmoid(acc)
    o_ref[...] = acc.astype(o_ref.dtype)


def matmul(x, w, layer=None, *, tm, tn, out_dtype, act=None, w_rows=None, name="matmul"):
    M, K = x.shape
    if w_rows is not None:
        first, N = w_rows
        w_spec = pl.BlockSpec((pl.Element(tn), pl.Element(K)),
                              lambda i, j: (pl.multiple_of(first + j * tn, 16), 0))
    elif layer is None:
        N = w.shape[1]
        w_spec = pl.BlockSpec((K, tn), lambda i, j: (0, j))
    else:
        N = w.shape[2]
        w_spec = pl.BlockSpec((None, K, tn), lambda i, j: (layer, 0, j))
    return pl.pallas_call(
        functools.partial(_mm_kernel, act=act, w_transposed=w_rows is not None),
        grid=(M // tm, N // tn),
        in_specs=[pl.BlockSpec((tm, K), lambda i, j: (i, 0)), w_spec],
        out_specs=pl.BlockSpec((tm, tn), lambda i, j: (i, j)),
        out_shape=jax.ShapeDtypeStruct((M, N), out_dtype),
        compiler_params=_cparams("parallel", "parallel"),
        name=name,
    )(x, w)


def _mm_swiglu_kernel(x_ref, wa_ref, wb_ref, *rest, n_casts):
    src_refs, o_ref, dst_refs = rest[:n_casts], rest[n_casts], rest[n_casts + 1:]
    x = x_ref[...]
    a = jnp.dot(x, wa_ref[...].astype(BF16), preferred_element_type=F32)
    b = jnp.dot(x, wb_ref[...].astype(BF16), preferred_element_type=F32)
    o_ref[...] = (jax.nn.silu(a) * b).astype(o_ref.dtype)
    for src, dst in zip(src_refs, dst_refs):
        dst[...] = src[...].astype(dst.dtype)


def matmul_swiglu(x, w13, layer, casts, *, tm, tn):
    M, K = x.shape
    N = w13.shape[2] // 2
    nj = N // tn
    steps = (M // tm) * nj
    cast_in, cast_out, cast_shape = [], [], []
    for w in casts:
        rows, cols = w.shape[1:]
        rb = 16 * pl.cdiv(pl.cdiv(rows, 16), steps)
        last = pl.cdiv(rows, rb) - 1

        def blk(i, j, last=last):
            return jnp.minimum(i * nj + j, last)

        cast_in.append(pl.BlockSpec((None, rb, cols), lambda i, j, blk=blk: (layer, blk(i, j), 0)))
        cast_out.append(pl.BlockSpec((rb, cols), lambda i, j, blk=blk: (blk(i, j), 0)))
        cast_shape.append(jax.ShapeDtypeStruct((rows, cols), BF16))
    out = pl.pallas_call(
        functools.partial(_mm_swiglu_kernel, n_casts=len(casts)),
        grid=(M // tm, nj),
        in_specs=[pl.BlockSpec((tm, K), lambda i, j: (i, 0), pipeline_mode=pl.Buffered(1)),
                  pl.BlockSpec((None, K, tn), lambda i, j: (layer, 0, j)),
                  pl.BlockSpec((None, K, tn), lambda i, j: (layer, 0, nj + j))] + cast_in,
        out_specs=[pl.BlockSpec((tm, tn), lambda i, j: (i, j))] + cast_out,
        out_shape=[jax.ShapeDtypeStruct((M, N), BF16)] + cast_shape,
        compiler_params=_cparams("arbitrary", "arbitrary"),
        name="ffn_up_swiglu",
    )(x, w13, w13, *casts)
    return out[0], out[1:]


def _rope_block(a, c, s):
    return a * c + pltpu.roll(a, 64, axis=1) * s


def _mm_qrope_kernel(x_ref, w_ref, c_ref, s_ref, o_ref, *, heads, scale):
    acc = jnp.dot(x_ref[...], w_ref[...], preferred_element_type=F32) * scale
    c = c_ref[...]
    s = s_ref[...]
    for hb in range(heads):
        lo = hb * MLA_QPAD
        o_ref[:, lo:lo + 128] = acc[:, lo:lo + 128].astype(o_ref.dtype)
        o_ref[:, lo + 128:lo + 256] = _rope_block(acc[:, lo + 128:lo + 256], c, s).astype(o_ref.dtype)


def matmul_qrope(x, w, layer, cos_t, sin_t, *, tm, heads_per_tile, scale):
    M, K = x.shape
    N = w.shape[2]
    tn = heads_per_tile * MLA_QPAD
    tab = pl.BlockSpec((tm, 128), lambda i, j: (i, 0))
    return pl.pallas_call(
        functools.partial(_mm_qrope_kernel, heads=heads_per_tile, scale=scale),
        grid=(M // tm, N // tn),
        in_specs=[pl.BlockSpec((tm, K), lambda i, j: (i, 0)),
                  pl.BlockSpec((None, K, tn), lambda i, j: (layer, 0, j)), tab, tab],
        out_specs=pl.BlockSpec((tm, tn), lambda i, j: (i, j)),
        out_shape=jax.ShapeDtypeStruct((M, N), BF16),
        compiler_params=_cparams("parallel", "parallel"),
        name="mla_q_proj_rope",
    )(x, w, cos_t, sin_t)


def _krope_kernel(x_ref, c_ref, s_ref, o_ref):
    o_ref[...] = _rope_block(x_ref[...], c_ref[...], s_ref[...]).astype(o_ref.dtype)


def k_rope(p, cos_t, sin_t, *, tm):
    M = p.shape[0]
    tab = pl.BlockSpec((tm, 128), lambda i: (i, 0))
    return pl.pallas_call(
        _krope_kernel,
        grid=(M // tm,),
        in_specs=[pl.BlockSpec((tm, 128), lambda i: (i, P_KR // 128)), tab, tab],
        out_specs=tab,
        out_shape=jax.ShapeDtypeStruct((M, 128), BF16),
        compiler_params=_cparams("parallel"),
        name="mla_k_rope",
    )(p, cos_t, sin_t)


def _merge_kernel(o_ref, hs_ref, dn_ref, wo_ref, wv_ref, wg_ref, wd_ref,
                  g0_ref, g1_ref, g2_ref, out_ref):
    hs = hs_ref[...]
    y_mla = jnp.dot(o_ref[...], wo_ref[...], preferred_element_type=F32)
    val = jnp.dot(hs, wv_ref[...], preferred_element_type=F32)
    gate = jnp.dot(hs, wg_ref[...], preferred_element_type=F32)
    y_dn = jnp.dot(dn_ref[...], wd_ref[...], preferred_element_type=F32)
    merged = (g0_ref[...].astype(F32) * y_mla
              + g1_ref[...].astype(F32) * (val * jax.nn.sigmoid(gate))
              + g2_ref[...].astype(F32) * y_dn)
    out_ref[...] = merged.astype(out_ref.dtype)


def merge_branches(o_mla, h_s5, o_dn, w_o, w_glu, w_dn_o, gates, layer, *, tm, tn):
    M = o_mla.shape[0]
    D = w_o.shape[2]
    nj = D // tn

    def rows(width):
        return pl.BlockSpec((tm, width), lambda i, j: (i, 0))

    def cols(kdim, off):
        return pl.BlockSpec((None, kdim, tn), lambda i, j: (layer, 0, off * nj + j))

    def gate(b):
        return pl.BlockSpec((tm, tn), lambda i, j: (i, b * nj + j))

    return pl.pallas_call(
        _merge_kernel,
        grid=(M // tm, nj),
        in_specs=[rows(MLA_WIDTH), rows(S5_WIDTH), rows(DN_WIDTH),
                  cols(MLA_WIDTH, 0), cols(S5_WIDTH, 0), cols(S5_WIDTH, 1), cols(DN_WIDTH, 0),
                  gate(0), gate(1), gate(2)],
        out_specs=pl.BlockSpec((tm, tn), lambda i, j: (i, j)),
        out_shape=jax.ShapeDtypeStruct((M, D), BF16),
        compiler_params=_cparams("parallel", "parallel"),
        name="merge_branches",
    )(o_mla, h_s5, o_dn, w_o, w_glu, w_glu, w_dn_o, gates, gates, gates)


_NEG = -1e30


def _chunk_id(pos):
    return (pos + (CHUNK - N_META)) >> 6


ATTN_HEADS_PER_STEP = 4


def _attn_kernel(q_ref, kn_ref, kr_ref, v_ref, o_ref, *, tq, seq_pad):
    G = ATTN_HEADS_PER_STEP
    tk = tq
    tkm = tq + 128
    q0 = pl.program_id(2) * tq
    qs = [q_ref[:, g * MLA_QPAD:(g + 1) * MLA_QPAD] for g in range(G)]
    full_end = (_chunk_id(q0) + 1) * CHUNK - (CHUNK - N_META)
    n_full = full_end // tk

    def step(ks, width, carry, mask):
        ms, accs = carry
        k_rope = kr_ref[pl.ds(ks, width), :]
        ones = jnp.ones((width, 128), BF16)

        def scores(g):
            k = jnp.concatenate([kn_ref[pl.ds(ks, width), g * 128:(g + 1) * 128], k_rope], axis=1)
            s = lax.dot_general(qs[g], k, (((1,), (1,)), ((), ())), preferred_element_type=F32)
            return s if mask is None else jnp.where(mask, s, _NEG)

        def weights(g, s):
            m_new = jnp.maximum(ms[g], jnp.max(s, axis=-1, keepdims=True))
            return m_new, jnp.exp2(s - m_new).astype(BF16), jnp.exp2(ms[g] - m_new)

        def update(g, p, alpha):
            v1 = jnp.concatenate([v_ref[pl.ds(ks, width), g * 128:(g + 1) * 128], ones], axis=1)
            return alpha * accs[g] + jnp.dot(p, v1, preferred_element_type=F32)

        ss = [scores(g) for g in range(G)]
        m_new, ps, alphas = zip(*[weights(g, ss[g]) for g in range(G)])
        return list(m_new), [update(g, ps[g], alphas[g]) for g in range(G)]

    def full_steps(lo, hi, width, carry):
        return lax.fori_loop(
            lo, hi, lambda j, c: step(pl.multiple_of(j * width, width), width, c, None), carry)

    init = ([jnp.full((tq, 1), _NEG, F32)] * G, [jnp.zeros((tq, 2 * MLA_V), F32)] * G)
    carry = full_steps(0, n_full // 4, 4 * tk, init)
    carry = full_steps((n_full // 4) * 2, n_full // 2, 2 * tk, carry)
    carry = full_steps(n_full - n_full % 2, n_full, tk, carry)

    first = n_full * tk
    ks = pl.multiple_of(jnp.minimum(first, seq_pad - tkm), 128)
    kpos = ks + lax.broadcasted_iota(jnp.int32, (1, tkm), 1)
    cq = _chunk_id(q0 + lax.broadcasted_iota(jnp.int32, (tq, 1), 0))
    mask = jnp.logical_and(cq >= _chunk_id(kpos), kpos >= first)
    _, accs = step(ks, tkm, carry, mask)
    for g in range(G):
        o_ref[:, g * MLA_V:(g + 1) * MLA_V] = (
            accs[g][:, :MLA_V] / accs[g][:, MLA_V:]).astype(o_ref.dtype)


def mla_attention(q, kv, kr, *, batch, seq_pad, tq):
    M = q.shape[0]
    nq = seq_pad // tq
    G = ATTN_HEADS_PER_STEP
    ng = MLA_HEADS // G
    assert tq + 128 <= seq_pad and tq > CHUNK
    return pl.pallas_call(
        functools.partial(_attn_kernel, tq=tq, seq_pad=seq_pad),
        grid=(batch, ng, nq),
        in_specs=[pl.BlockSpec((tq, G * MLA_QPAD), lambda b, h, i: (b * nq + i, h)),
                  pl.BlockSpec((seq_pad, G * 128), lambda b, h, i: (b, h)),
                  pl.BlockSpec((seq_pad, 128), lambda b, h, i: (b, 0)),
                  pl.BlockSpec((seq_pad, G * 128), lambda b, h, i: (b, ng + h))],
        out_specs=pl.BlockSpec((tq, G * MLA_V), lambda b, h, i: (b * nq + i, h)),
        out_shape=jax.ShapeDtypeStruct((M, MLA_WIDTH), BF16),
        compiler_params=_cparams("parallel", "parallel", "parallel"),
        name="mla_attention",
    )(q, kv, kr, kv)


def _s5_kernel(u_ref, br_ref, bi_ref, cr_ref, ci_ref, d_ref, ast_ref, pw_ref, o_ref,
               xr_s, xi_s, car_s, *, tt):
    n = S5_GPB * S5_STATE

    @pl.when(pl.program_id(2) == 0)
    def _():
        car_s[...] = jnp.zeros_like(car_s)

    u = u_ref[...]
    ub = u.astype(BF16)
    xr = jnp.dot(ub, br_ref[...], preferred_element_type=F32).reshape(tt // 8, 8, n)
    xi = jnp.dot(ub, bi_ref[...], preferred_element_type=F32).reshape(tt // 8, 8, n)
    for si in range(3):
        ar = ast_ref[2 * si]
        ai = ast_ref[2 * si + 1]
        sr = pltpu.roll(xr, 1 << si, axis=1)
        sm = pltpu.roll(xi, 1 << si, axis=1)
        xr, xi = xr + (ar * sr - ai * sm), xi + (ar * sm + ai * sr)
    xr_s[...] = xr.reshape(tt, n)
    xi_s[...] = xi.reshape(tt, n)
    pr = pw_ref[0]
    pi = pw_ref[1]

    def body(i, carry):
        cr, ci = carry
        sl = pl.ds(pl.multiple_of(i * 8, 8), 8)
        a = xr_s[sl, :] + (pr * cr - pi * ci)
        b = xi_s[sl, :] + (pr * ci + pi * cr)
        xr_s[sl, :] = a
        xi_s[sl, :] = b
        return (jnp.broadcast_to(a[7:8, :], (8, n)), jnp.broadcast_to(b[7:8, :], (8, n)))

    cr, ci = lax.fori_loop(0, tt // 8, body, (car_s[0], car_s[1]))
    car_s[0] = cr
    car_s[1] = ci
    y = (jnp.dot(xr_s[...].astype(BF16), cr_ref[...], preferred_element_type=F32)
         - jnp.dot(xi_s[...].astype(BF16), ci_ref[...], preferred_element_type=F32)
         + d_ref[...] * u)
    o_ref[...] = jax.nn.gelu(y).astype(o_ref.dtype)


def s5_scan(p, s5c, *, batch, seq_pad, tt):
    M = p.shape[0]
    nt = seq_pad // tt
    cw = S5_GPB * S5_GROUP
    n = S5_GPB * S5_STATE
    ublk = P_S5 // cw
    return pl.pallas_call(
        functools.partial(_s5_kernel, tt=tt),
        grid=(batch, S5_GB, nt),
        in_specs=[pl.BlockSpec((tt, cw), lambda b, g, t: (b * nt + t, ublk + g)),
                  pl.BlockSpec((None, cw, n), lambda b, g, t: (g, 0, 0)),
                  pl.BlockSpec((None, cw, n), lambda b, g, t: (g, 0, 0)),
                  pl.BlockSpec((None, n, cw), lambda b, g, t: (g, 0, 0)),
                  pl.BlockSpec((None, n, cw), lambda b, g, t: (g, 0, 0)),
                  pl.BlockSpec((None, 1, cw), lambda b, g, t: (g, 0, 0)),
                  pl.BlockSpec((None, 6, 8, n), lambda b, g, t: (g, 0, 0, 0)),
                  pl.BlockSpec((None, 2, 8, n), lambda b, g, t: (g, 0, 0, 0))],
        out_specs=pl.BlockSpec((tt, cw), lambda b, g, t: (b * nt + t, g)),
        out_shape=jax.ShapeDtypeStruct((M, S5_WIDTH), BF16),
        scratch_shapes=[pltpu.VMEM((tt, n), F32), pltpu.VMEM((tt, n), F32),
                        pltpu.VMEM((2, 8, n), F32)],
        compiler_params=_cparams("parallel", "parallel", "arbitrary"),
        name="s5_scan",
    )(p, s5c["br"], s5c["bi"], s5c["cr"], s5c["ci"], s5c["d"], s5c["ast"], s5c["pw"])


def s5_constants(a_re, a_im, log_dt, b_re, b_im, c_re, c_im, d):
    ar, ai = a_re.astype(F32), a_im.astype(F32)
    delta = jnp.exp(log_dt.astype(F32))[:, None]
    mag = jnp.exp(ar * delta)
    abar_r, abar_i = mag * jnp.cos(ai * delta), mag * jnp.sin(ai * delta)
    den = ar * ar + ai * ai
    zr = ((abar_r - 1.0) * ar + abar_i * ai) / den
    zi = (abar_i * ar - (abar_r - 1.0) * ai) / den
    br, bi = b_re.astype(F32), b_im.astype(F32)
    bbar_r = zr[..., None] * br - zi[..., None] * bi
    bbar_i = zr[..., None] * bi + zi[..., None] * br

    eye = jnp.eye(S5_GPB, dtype=F32)

    def in_blockdiag(m):
        m = m.reshape(S5_GB, S5_GPB, S5_STATE, S5_GROUP)
        bd = jnp.einsum("bgpc,gh->bgchp", m, eye)
        return bd.reshape(S5_GB, S5_GPB * S5_GROUP, S5_GPB * S5_STATE).astype(BF16)

    def out_blockdiag(m):
        m = m.reshape(S5_GB, S5_GPB, S5_GROUP, S5_STATE)
        bd = jnp.einsum("bgcp,gh->bgphc", m, eye)
        return bd.reshape(S5_GB, S5_GPB * S5_STATE, S5_GPB * S5_GROUP).astype(BF16)

    def cmul(x, y):
        return x[0] * y[0] - x[1] * y[1], x[0] * y[1] + x[1] * y[0]

    n = S5_GPB * S5_STATE
    a1 = (abar_r.reshape(S5_GB, n), abar_i.reshape(S5_GB, n))
    a2 = cmul(a1, a1)
    a4 = cmul(a2, a2)
    row = jnp.arange(8)[None, :, None]
    ast = []
    for s, a in ((1, a1), (2, a2), (4, a4)):
        for part in a:
            ast.append(jnp.where(row >= s, part[:, None, :], 0.0))
    ast = jnp.stack(ast, axis=1)
    pows = [a1]
    for _ in range(7):
        pows.append(cmul(pows[-1], a1))
    pw = jnp.stack([jnp.stack([p[0] for p in pows], axis=1),
                    jnp.stack([p[1] for p in pows], axis=1)], axis=1)
    return {
        "br": in_blockdiag(bbar_r), "bi": in_blockdiag(bbar_i),
        "cr": out_blockdiag(c_re.astype(F32)), "ci": out_blockdiag(c_im.astype(F32)),
        "d": d.astype(F32).reshape(S5_GB, 1, S5_GPB * S5_GROUP),
        "ast": ast, "pw": pw,
    }


def _dn_gates_kernel(ab_ref, alog_ref, dtb_ref, gcum_ref, beta_ref):
    ab = ab_ref[...]
    g = -jnp.exp(alog_ref[...]) * jax.nn.softplus(ab[:DN_HEADS] + dtb_ref[...])
    beta_ref[...] = jax.nn.sigmoid(ab[DN_HEADS:])
    lane = lax.broadcasted_iota(jnp.int32, g.shape, 1) % CHUNK
    s = 1
    while s < CHUNK:
        g = g + jnp.where(lane >= s, pltpu.roll(g, s, axis=1), 0.0)
        s *= 2
    gcum_ref[...] = g


def dn_gates(ab_t, a_log, dt_bias):
    B, _, Lp = ab_t.shape
    col = pl.BlockSpec((DN_HEADS, 1), lambda b: (0, 0))
    out = pl.BlockSpec((None, DN_HEADS, Lp), lambda b: (b, 0, 0))
    return pl.pallas_call(
        _dn_gates_kernel,
        grid=(B,),
        in_specs=[pl.BlockSpec((None, 2 * DN_HEADS, Lp), lambda b: (b, 0, 0)), col, col],
        out_specs=[out, out],
        out_shape=[jax.ShapeDtypeStruct((B, DN_HEADS, Lp), F32)] * 2,
        compiler_params=_cparams("parallel"),
        name="dn_gates",
    )(ab_t, a_log.astype(F32).reshape(DN_HEADS, 1), dt_bias.astype(F32).reshape(DN_HEADS, 1))


def _dot(a, b):
    return jnp.dot(a, b, preferred_element_type=F32)


def _dot_nt(a, b):
    return lax.dot_general(a, b, (((1,), (1,)), ((), ())), preferred_element_type=F32)


def _dot_tn(a, b):
    return lax.dot_general(a, b, (((0,), (0,)), ((), ())), preferred_element_type=F32)


def _l2(x):
    return x * lax.rsqrt(jnp.sum(x * x, axis=-1, keepdims=True) + EPS)


def _dn_prep_kernel(q_ref, k_ref, v_ref, cwq_ref, cwk_ref, cwv_ref, bcol_ref, gcol_ref, grow_ref,
                    u_ref, w_ref, qg_ref, kd_ref, qk_ref, dl_ref, *, n_chunks, unroll):
    C = CHUNK

    def conv_silu(x_ref, cw_ref, c, sl):
        cur = x_ref[sl, :]
        halo = x_ref[pl.ds(pl.multiple_of(jnp.maximum(c * C - 8, 0), 8), 8), :]
        xc = jnp.concatenate([jnp.where(c > 0, halo, 0.0), cur], axis=0)
        acc = cur * cw_ref[DN_CONV - 1:DN_CONV, :]
        for j in range(DN_CONV - 1):
            lo = 8 - (DN_CONV - 1) + j
            acc = acc + xc[lo:lo + C, :] * cw_ref[j:j + 1, :]
        return jax.nn.silu(acc)

    ii = lax.broadcasted_iota(jnp.int32, (C, C), 0)
    jj = lax.broadcasted_iota(jnp.int32, (C, C), 1)
    tri = ii >= jj
    strict = ii > jj
    eye = (ii == jj).astype(F32)

    def body(i, carry):
        U = range(unroll)
        cs = [i * unroll + j for j in U]
        sls = [pl.ds(pl.multiple_of(c * C, C), C) for c in cs]
        q = [_l2(conv_silu(q_ref, cwq_ref, cs[j], sls[j])) * (DN_DK ** -0.5) for j in U]
        k = [_l2(conv_silu(k_ref, cwk_ref, cs[j], sls[j])) for j in U]
        v = [conv_silu(v_ref, cwv_ref, cs[j], sls[j]) for j in U]
        beta = [bcol_ref[sl, :] for sl in sls]
        gc = [gcol_ref[sl, :] for sl in sls]
        decay = [jnp.exp(jnp.where(tri, gc[j] - grow_ref[pl.ds(cs[j], 1), :], -jnp.inf)) for j in U]
        kb = [k[j] * beta[j] for j in U]
        a_mat = [jnp.where(strict, _dot_nt(kb[j], k[j]) * decay[j], 0.0) for j in U]
        t_inv = [eye - a for a in a_mat]
        pw = a_mat
        for _ in range(5):
            pw = [_dot(x, x) for x in pw]
            t_inv = [t + _dot(t, x) for t, x in zip(t_inv, pw)]
        u = [_dot(t_inv[j], v[j] * beta[j]) for j in U]
        w = [_dot(t_inv[j], kb[j] * jnp.exp(gc[j])) for j in U]
        qk = [_dot_nt(q[j].astype(BF16), k[j].astype(BF16)) * decay[j] for j in U]
        for j in U:
            sl = sls[j]
            u_ref[sl, :] = u[j]
            w_ref[sl, :] = w[j].astype(w_ref.dtype)
            qk_ref[sl, :] = qk[j].astype(qk_ref.dtype)
            qg_ref[sl, :] = (q[j] * jnp.exp(gc[j])).astype(qg_ref.dtype)
            g_last = gc[j][C - 1:C, :]
            kd_ref[sl, :] = (k[j] * jnp.exp(g_last - gc[j])).astype(kd_ref.dtype)
            dl_ref[pl.ds(cs[j], 1), :] = jnp.broadcast_to(jnp.exp(g_last), (1, dl_ref.shape[1]))
        return carry

    lax.fori_loop(0, n_chunks // unroll, body, 0)


def dn_prep(p, conv_w, beta_col, gcum_col, gcum_row, *, batch, seq_pad):
    M = p.shape[0]
    H = DN_HEADS
    nc = seq_pad // CHUNK
    c0 = P_DNQKV // 128

    def head(off):
        return pl.BlockSpec((seq_pad, 128), lambda b, h: (b, off + h))

    def taps(off):
        return pl.BlockSpec((DN_CONV, 128), lambda b, h: (0, off + h))

    colspec = pl.BlockSpec((None, None, seq_pad, 1), lambda b, h: (b, h, 0, 0))
    wide = jax.ShapeDtypeStruct((M, DN_WIDTH), BF16)
    conv_w = conv_w.astype(F32)
    return pl.pallas_call(
        functools.partial(_dn_prep_kernel, n_chunks=nc, unroll=_pick(nc, (11, 6, 4, 3, 2, 1))),
        grid=(batch, H),
        in_specs=[head(c0), head(c0 + H), head(c0 + 2 * H), taps(0), taps(H), taps(2 * H),
                  colspec, colspec,
                  pl.BlockSpec((None, None, nc, CHUNK), lambda b, h: (b, h, 0, 0))],
        out_specs=[head(0), head(0), head(0), head(0),
                   pl.BlockSpec((None, None, seq_pad, CHUNK), lambda b, h: (b, h, 0, 0)),
                   pl.BlockSpec((None, None, nc, 128), lambda b, h: (b, h, 0, 0))],
        out_shape=[jax.ShapeDtypeStruct((M, DN_WIDTH), F32), wide, wide, wide,
                   jax.ShapeDtypeStruct((batch, H, seq_pad, CHUNK), BF16),
                   jax.ShapeDtypeStruct((batch, H, nc, 128), F32)],
        compiler_params=_cparams("parallel", "parallel"),
        name="dn_prep",
    )(p, p, p, conv_w, conv_w, conv_w, beta_col, gcum_col, gcum_row)


def _dn_scan_kernel(u_ref, w_ref, qg_ref, kd_ref, qk_ref, dl_ref, z_ref, gn_ref, o_ref, s_ref,
                    *, chunks_per_tile):
    C = CHUNK

    @pl.when(pl.program_id(1) == 0)
    def _():
        s_ref[...] = jnp.zeros_like(s_ref)

    gn = gn_ref[...]

    def body(c, carry):
        sl = pl.ds(pl.multiple_of(c * C, C), C)
        H = range(DN_HEADS)
        hs = [slice(j * DN_DV, (j + 1) * DN_DV) for j in H]
        S = [s_ref[j] for j in H]
        Sb = [s.astype(BF16) for s in S]
        v_new = [u_ref[sl, hs[j]] - _dot(w_ref[sl, hs[j]], Sb[j]) for j in H]
        o_inter = [_dot(qg_ref[sl, hs[j]], Sb[j]) for j in H]
        vb = [v.astype(BF16) for v in v_new]
        o = [o_inter[j] + _dot(qk_ref[j, sl, :], vb[j]) for j in H]
        s_upd = [_dot_tn(kd_ref[sl, hs[j]], vb[j]) for j in H]
        for j in H:
            s_ref[j] = S[j] * dl_ref[j, pl.ds(c, 1), :] + s_upd[j]
            o_ref[sl, hs[j]] = (_rms(o[j], gn) * jax.nn.silu(z_ref[sl, hs[j]])).astype(o_ref.dtype)
        return carry

    lax.fori_loop(0, chunks_per_tile, body, 0)


def dn_scan(u, w, qg, kd, qk, dl, p, out_norm_g, *, batch, seq_pad):
    M = p.shape[0]
    H = DN_HEADS
    nc = seq_pad // CHUNK
    cpt = _pick(nc, (11, 6, 4, 3, 2, 1))
    nt = nc // cpt
    tt = cpt * CHUNK
    dl = dl.reshape(batch, H, nt, cpt, 128)

    def rows(col_block=0):
        return pl.BlockSpec((tt, DN_WIDTH), lambda b, t: (b * nt + t, col_block))

    return pl.pallas_call(
        functools.partial(_dn_scan_kernel, chunks_per_tile=cpt),
        grid=(batch, nt),
        in_specs=[rows(), rows(), rows(), rows(),
                  pl.BlockSpec((None, H, tt, CHUNK), lambda b, t: (b, 0, t, 0)),
                  pl.BlockSpec((None, H, None, cpt, 128), lambda b, t: (b, 0, t, 0, 0)),
                  rows(P_DNZ // DN_WIDTH),
                  pl.BlockSpec((1, DN_DV), lambda b, t: (0, 0))],
        out_specs=rows(),
        out_shape=jax.ShapeDtypeStruct((M, DN_WIDTH), BF16),
        scratch_shapes=[pltpu.VMEM((H, DN_DK, DN_DV), F32)],
        compiler_params=_cparams("parallel", "arbitrary"),
        name="dn_scan",
    )(u, w, qg, kd, qk, dl, p, out_norm_g.astype(F32).reshape(1, DN_DV))


def _rot_cols(w):
    half = w.shape[-1] // 2
    return jnp.concatenate([-w[..., half:], w[..., :half]], axis=-1)


def _regroup_in_proj(wt):
    kr = wt[OFF_KR:OFF_S5]
    kr_rot = jnp.concatenate([-kr[MLA_ROPE // 2:], kr[:MLA_ROPE // 2]], axis=0)
    return jnp.concatenate([
        wt[OFF_Q:OFF_KV], wt[OFF_DN_Z:OFF_DN_A], wt[OFF_S5:OFF_DN_Z], wt[OFF_KV:OFF_KR],
        kr, kr_rot, wt[OFF_DN_A:OFF_GATE],
        jnp.zeros((128 - 2 * DN_HEADS, D_MODEL), BF16)], axis=0)


def _prep_mla(mla_w_uq, mla_w_ukv):
    depth = mla_w_uq.shape[0]
    uq = mla_w_uq.reshape(depth, MLA_Q_LORA, MLA_HEADS, MLA_QK)
    w_uq = jnp.concatenate([uq, _rot_cols(uq[..., MLA_NOPE:])], axis=-1)
    w_uq = w_uq.reshape(depth, MLA_Q_LORA, MLA_HEADS * MLA_QPAD).astype(BF16)
    ukv = mla_w_ukv.reshape(depth, MLA_KV_LORA, MLA_HEADS, 2, 128)
    w_ukv = jnp.transpose(ukv, (0, 1, 3, 2, 4)).reshape(depth, MLA_KV_LORA, 2 * MLA_HEADS * 128)
    return w_uq, w_ukv.astype(BF16)


def _rope_tables(seq_pad, batch):
    inv = ROPE_THETA ** (-jnp.arange(0, MLA_ROPE, 2, dtype=F32) / MLA_ROPE)
    ang = jnp.arange(seq_pad, dtype=F32)[:, None] * inv[None, :]
    zeros = jnp.zeros((seq_pad, 128 - MLA_ROPE), F32)
    cos_t = jnp.concatenate([jnp.cos(ang), jnp.cos(ang), zeros], axis=1)
    sin_t = jnp.concatenate([jnp.sin(ang), jnp.sin(ang), zeros], axis=1)
    return jnp.tile(cos_t, (batch, 1)), jnp.tile(sin_t, (batch, 1))


def _ffn(xn, w13, w2, layer, extra_casts, *, tm_up, tm):
    hid, (w2_bf16, *extra) = matmul_swiglu(xn, w13, layer, [w2] + extra_casts, tm=tm_up, tn=256)
    return matmul(hid, w2_bf16, tm=tm // 2, tn=512, out_dtype=BF16, name="ffn_down"), extra


def kernel(x, meta_tokens, sandwich_g, ffn1_w13, ffn1_w2, w_in, mla_q_norm_g, mla_kv_norm_g, mla_w_uq, mla_w_ukv, mla_w_o, s5_a_re, s5_a_im, s5_log_dt, s5_b_re, s5_b_im, s5_c_re, s5_c_im, s5_d, s5_w_glu, dn_conv_w, dn_a_log, dn_dt_bias, dn_out_norm_g, dn_w_o, w_out, ffn2_w13, ffn2_w2):
    B, seq, D = x.shape
    depth = w_in.shape[0]
    L = N_META + seq
    assert (L - N_META) % CHUNK == 0, "pad keys are hidden by the chunk mask only on a chunk boundary"
    Lp = -(-L // 128) * 128
    M = B * Lp
    tm = _pick(M, (1056, 768, 384, 128))
    tm_up = _pick(M, (2112, 1536, 768, 384, 128))
    tr = _pick(M, (176, 128))
    tt = _pick(Lp, (528, 384, 128))
    ta = _pick(Lp, tuple(c for c in (384, 128) if c + 128 <= Lp))

    meta = jnp.broadcast_to(meta_tokens[None].astype(x.dtype), (B, N_META, D))
    h = jnp.concatenate([meta, x, jnp.zeros((B, Lp - L, D), x.dtype)], axis=1).reshape(M, D)
    cos_t, sin_t = _rope_tables(Lp, B)

    w_in_t = jnp.swapaxes(w_in, 1, 2)
    w_uq, w_ukv = _prep_mla(mla_w_uq, mla_w_ukv)
    w_mla_o, w_glu, w_dn_o, w_mix = (t.astype(BF16) for t in (mla_w_o, s5_w_glu, dn_w_o, w_out))
    q_scale = MLA_QK ** -0.5 * math.log2(math.e)

    xn = rmsnorm_cast(h, sandwich_g[0, 0], col_block=0, width=D, tm=tr)
    for l in range(depth):
        g = sandwich_g[l]
        y, (w_in_b,) = _ffn(xn, ffn1_w13, ffn1_w2, l, [w_in_t], tm_up=tm_up, tm=tm)
        h, xn = resid_norm(h, y, g[1], g[2], coef=0.5, tm=tr)

        p = matmul(xn, _regroup_in_proj(w_in_b), tm=tm, tn=768, out_dtype=F32, w_rows=(0, P_WIDTH),
                   name="in_proj_small")
        gates = matmul(xn, w_in_b, tm=tm, tn=1024, out_dtype=BF16, act="sigmoid",
                       w_rows=(OFF_GATE, 3 * D), name="in_proj_gates")

        qn = rmsnorm_cast(p, mla_q_norm_g[l], col_block=P_Q // MLA_Q_LORA, width=MLA_Q_LORA, tm=tm)
        kvn = rmsnorm_cast(p, mla_kv_norm_g[l], col_block=P_KV // MLA_KV_LORA, width=MLA_KV_LORA, tm=tm)
        q = matmul_qrope(qn, w_uq, l, cos_t, sin_t, tm=tm, heads_per_tile=4, scale=q_scale)
        kv = matmul(kvn, w_ukv, l, tm=tm, tn=1024, out_dtype=BF16, name="mla_kv_proj")
        kr = k_rope(p, cos_t, sin_t, tm=tm)
        o_mla = mla_attention(q, kv, kr, batch=B, seq_pad=Lp, tq=ta)

        s5c = s5_constants(s5_a_re[l], s5_a_im[l], s5_log_dt[l], s5_b_re[l], s5_b_im[l],
                           s5_c_re[l], s5_c_im[l], s5_d[l])
        h_s5 = s5_scan(p, s5c, batch=B, seq_pad=Lp, tt=tt)

        ab_t = jnp.transpose(p[:, P_AB:P_AB + 2 * DN_HEADS].reshape(B, Lp, 2 * DN_HEADS), (0, 2, 1))
        gcum, beta = dn_gates(ab_t, dn_a_log[l], dn_dt_bias[l])
        u, w, qg, kd, qk, dl = dn_prep(p, dn_conv_w[l], beta[..., None], gcum[..., None],
                                       gcum.reshape(B, DN_HEADS, Lp // CHUNK, CHUNK),
                                       batch=B, seq_pad=Lp)
        o_dn = dn_scan(u, w, qg, kd, qk, dl, p, dn_out_norm_g[l], batch=B, seq_pad=Lp)

        merged = merge_branches(o_mla, h_s5, o_dn, w_mla_o, w_glu, w_dn_o, gates, l, tm=tm, tn=512)
        mix = matmul(merged, w_mix, l, tm=tm, tn=1024, out_dtype=BF16, name="out_proj")
        h, xn = resid_norm(h, mix, g[3], g[4], coef=1.0, tm=tr)

        y, _ = _ffn(xn, ffn2_w13, ffn2_w2, l, [], tm_up=tm_up, tm=tm)
        if l + 1 < depth:
            h, xn = resid_norm(h, y, g[5], sandwich_g[l + 1, 0], coef=0.5, tm=tr)
    out = resid_out(h, y, g[5], coef=0.5, batch=B, seq_pad=Lp, first=N_META, count=seq,
                    tm=_pick(seq, (256, 128, 64)))
    return out.reshape(B, seq, D)
```

```python
import functools
import math

import jax
import jax.numpy as jnp
from jax import lax
from jax.experimental import pallas as pl
from jax.experimental.pallas import tpu as pltpu

F32 = jnp.float32
BF16 = jnp.bfloat16

D_MODEL = 4096
CHUNK = 64
N_META = 16
EPS = 1e-6

MLA_HEADS = 16
MLA_Q_LORA = 1024
MLA_KV_LORA = 512
MLA_NOPE = 128
MLA_ROPE = 64
MLA_V = 128
MLA_QK = MLA_NOPE + MLA_ROPE
MLA_WIDTH = MLA_HEADS * MLA_V
MLA_QPAD = 256
ROPE_THETA = 10000.0

S5_WIDTH = 1024
S5_GROUP = 16
S5_GROUPS = S5_WIDTH // S5_GROUP
S5_STATE = 64
S5_GB = 4
S5_GPB = S5_GROUPS // S5_GB

DN_HEADS = 8
DN_DK = 128
DN_DV = 128
DN_QK = DN_HEADS * DN_DK
DN_WIDTH = DN_HEADS * DN_DV
DN_CONV = 4

FFN_HIDDEN = 11008

OFF_Q = 0
OFF_KV = OFF_Q + MLA_Q_LORA
OFF_KR = OFF_KV + MLA_KV_LORA
OFF_S5 = OFF_KR + MLA_ROPE
OFF_DN_QKV = OFF_S5 + S5_WIDTH
OFF_DN_Z = OFF_DN_QKV + 2 * DN_QK + DN_WIDTH
OFF_DN_A = OFF_DN_Z + DN_WIDTH
OFF_DN_B = OFF_DN_A + DN_HEADS
OFF_GATE = OFF_DN_B + DN_HEADS

P_Q = 0
P_DNZ = P_Q + MLA_Q_LORA
P_S5 = P_DNZ + DN_WIDTH
P_DNQKV = P_S5 + S5_WIDTH
P_KV = P_DNQKV + 3 * DN_QK
P_KR = P_KV + MLA_KV_LORA
P_AB = P_KR + 128
P_WIDTH = P_AB + 128

VMEM_LIMIT_BYTES = 56 * 1024 * 1024


def _cparams(*sem):
    return pltpu.CompilerParams(dimension_semantics=sem, vmem_limit_bytes=VMEM_LIMIT_BYTES)


def _pick(n, candidates):
    for c in candidates:
        if n % c == 0:
            return c
    raise ValueError(f"no tile in {candidates} divides {n}")


def _rms(x, g):
    return x * lax.rsqrt(jnp.mean(x * x, axis=-1, keepdims=True) + EPS) * g


def _rmsnorm_kernel(x_ref, g_ref, o_ref):
    o_ref[...] = _rms(x_ref[...], g_ref[...]).astype(o_ref.dtype)


def rmsnorm_cast(x, g, *, col_block, width, tm):
    M = x.shape[0]
    return pl.pallas_call(
        _rmsnorm_kernel,
        grid=(M // tm,),
        in_specs=[pl.BlockSpec((tm, width), lambda i: (i, col_block)),
                  pl.BlockSpec((1, width), lambda i: (0, 0))],
        out_specs=pl.BlockSpec((tm, width), lambda i: (i, 0)),
        out_shape=jax.ShapeDtypeStruct((M, width), BF16),
        compiler_params=_cparams("parallel"),
        name="rmsnorm_cast",
    )(x, g.reshape(1, width).astype(F32))


def _resid_norm_kernel(h_ref, y_ref, gpost_ref, gpre_ref, ho_ref, xn_ref, *, coef):
    h = h_ref[...] + coef * _rms(y_ref[...].astype(F32), gpost_ref[...])
    ho_ref[...] = h
    xn_ref[...] = _rms(h, gpre_ref[...]).astype(xn_ref.dtype)


def _resid_kernel(h_ref, y_ref, gpost_ref, ho_ref, *, coef):
    ho_ref[...] = h_ref[...] + coef * _rms(y_ref[...].astype(F32), gpost_ref[...])


def resid_out(h, y, g_post, *, coef, batch, seq_pad, first, count, tm):
    D = h.shape[1]
    nt = count // tm

    def rows(b, t):
        return pl.multiple_of(b * seq_pad + first + t * tm, 16), 0

    row = pl.BlockSpec((pl.Element(tm), pl.Element(D)), rows)
    return pl.pallas_call(
        functools.partial(_resid_kernel, coef=coef),
        grid=(batch, nt),
        in_specs=[row, row, pl.BlockSpec((1, D), lambda b, t: (0, 0))],
        out_specs=pl.BlockSpec((tm, D), lambda b, t: (b * nt + t, 0)),
        out_shape=jax.ShapeDtypeStruct((batch * count, D), F32),
        compiler_params=_cparams("parallel", "parallel"), name="resid_out",
    )(h, y, g_post.reshape(1, D))


def resid_norm(h, y, g_post, g_pre, *, coef, tm):
    M, D = h.shape
    row = pl.BlockSpec((tm, D), lambda i: (i, 0))
    gain = pl.BlockSpec((1, D), lambda i: (0, 0))
    return pl.pallas_call(
        functools.partial(_resid_norm_kernel, coef=coef),
        grid=(M // tm,), in_specs=[row, row, gain, gain], out_specs=[row, row],
        out_shape=[jax.ShapeDtypeStruct((M, D), F32), jax.ShapeDtypeStruct((M, D), BF16)],
        input_output_aliases={0: 0},
        compiler_params=_cparams("parallel"), name="resid_norm",
    )(h, y, g_post.reshape(1, D), g_pre.reshape(1, D))


def _mm_kernel(x_ref, w_ref, o_ref, *, act, w_transposed):
    if w_transposed:
        acc = lax.dot_general(x_ref[...], w_ref[...], (((1,), (1,)), ((), ())),
                              preferred_element_type=F32)
    else:
        acc = jnp.dot(x_ref[...], w_ref[...], preferred_element_type=F32)
    if act == "sigmoid":
        acc = jax.nn.sigmoid(acc)
    o_ref[...] = acc.astype(o_ref.dtype)


def matmul(x, w, layer=None, *, tm, tn, out_dtype, act=None, w_rows=None, name="matmul"):
    M, K = x.shape
    if w_rows is not None:
        first, N = w_rows
        w_spec = pl.BlockSpec((pl.Element(tn), pl.Element(K)),
                              lambda i, j: (pl.multiple_of(first + j * tn, 16), 0))
    elif layer is None:
        N = w.shape[1]
        w_spec = pl.BlockSpec((K, tn), lambda i, j: (0, j))
    else:
        N = w.shape[2]
        w_spec = pl.BlockSpec((None, K, tn), lambda i, j: (layer, 0, j))
    return pl.pallas_call(
        functools.partial(_mm_kernel, act=act, w_transposed=w_rows is not None),
        grid=(M // tm, N // tn),
        in_specs=[pl.BlockSpec((tm, K), lambda i, j: (i, 0)), w_spec],
        out_specs=pl.BlockSpec((tm, tn), lambda i, j: (i, j)),
        out_shape=jax.ShapeDtypeStruct((M, N), out_dtype),
        compiler_params=_cparams("parallel", "parallel"),
        name=name,
    )(x, w)


def _mm_swiglu_kernel(x_ref, wa_ref, wb_ref, *rest, n_casts):
    src_refs, o_ref, dst_refs = rest[:n_casts], rest[n_casts], rest[n_casts + 1:]
    x = x_ref[...]
    a = jnp.dot(x, wa_ref[...].astype(BF16), preferred_element_type=F32)
    b = jnp.dot(x, wb_ref[...].astype(BF16), preferred_element_type=F32)
    o_ref[...] = (jax.nn.silu(a) * b).astype(o_ref.dtype)
    for src, dst in zip(src_refs, dst_refs):
        dst[...] = src[...].astype(dst.dtype)


def matmul_swiglu(x, w13, layer, casts, *, tm, tn):
    M, K = x.shape
    N = w13.shape[2] // 2
    nj = N // tn
    steps = (M // tm) * nj
    cast_in, cast_out, cast_shape = [], [], []
    for w in casts:
        rows, cols = w.shape[1:]
        rb = 16 * pl.cdiv(pl.cdiv(rows, 16), steps)
        last = pl.cdiv(rows, rb) - 1

        def blk(i, j, last=last):
            return jnp.minimum(i * nj + j, last)

        cast_in.append(pl.BlockSpec((None, rb, cols), lambda i, j, blk=blk: (layer, blk(i, j), 0)))
        cast_out.append(pl.BlockSpec((rb, cols), lambda i, j, blk=blk: (blk(i, j), 0)))
        cast_shape.append(jax.ShapeDtypeStruct((rows, cols), BF16))
    out = pl.pallas_call(
        functools.partial(_mm_swiglu_kernel, n_casts=len(casts)),
        grid=(M // tm, nj),
        in_specs=[pl.BlockSpec((tm, K), lambda i, j: (i, 0), pipeline_mode=pl.Buffered(1)),
                  pl.BlockSpec((None, K, tn), lambda i, j: (layer, 0, j)),
                  pl.BlockSpec((None, K, tn), lambda i, j: (layer, 0, nj + j))] + cast_in,
        out_specs=[pl.BlockSpec((tm, tn), lambda i, j: (i, j))] + cast_out,
        out_shape=[jax.ShapeDtypeStruct((M, N), BF16)] + cast_shape,
        compiler_params=_cparams("arbitrary", "arbitrary"),
        name="ffn_up_swiglu",
    )(x, w13, w13, *casts)
    return out[0], out[1:]


def _rope_block(a, c, s):
    return a * c + pltpu.roll(a, 64, axis=1) * s


def _mm_qrope_kernel(x_ref, w_ref, c_ref, s_ref, o_ref, *, heads, scale):
    acc = jnp.dot(x_ref[...], w_ref[...], preferred_element_type=F32) * scale
    c = c_ref[...]
    s = s_ref[...]
    for hb in range(heads):
        lo = hb * MLA_QPAD
        o_ref[:, lo:lo + 128] = acc[:, lo:lo + 128].astype(o_ref.dtype)
        o_ref[:, lo + 128:lo + 256] = _rope_block(acc[:, lo + 128:lo + 256], c, s).astype(o_ref.dtype)


def matmul_qrope(x, w, layer, cos_t, sin_t, *, tm, heads_per_tile, scale):
    M, K = x.shape
    N = w.shape[2]
    tn = heads_per_tile * MLA_QPAD
    tab = pl.BlockSpec((tm, 128), lambda i, j: (i, 0))
    return pl.pallas_call(
        functools.partial(_mm_qrope_kernel, heads=heads_per_tile, scale=scale),
        grid=(M // tm, N // tn),
        in_specs=[pl.BlockSpec((tm, K), lambda i, j: (i, 0)),
                  pl.BlockSpec((None, K, tn), lambda i, j: (layer, 0, j)), tab, tab],
        out_specs=pl.BlockSpec((tm, tn), lambda i, j: (i, j)),
        out_shape=jax.ShapeDtypeStruct((M, N), BF16),
        compiler_params=_cparams("parallel", "parallel"),
        name="mla_q_proj_rope",
    )(x, w, cos_t, sin_t)


def _krope_kernel(x_ref, c_ref, s_ref, o_ref):
    o_ref[...] = _rope_block(x_ref[...], c_ref[...], s_ref[...]).astype(o_ref.dtype)


def k_rope(p, cos_t, sin_t, *, tm):
    M = p.shape[0]
    tab = pl.BlockSpec((tm, 128), lambda i: (i, 0))
    return pl.pallas_call(
        _krope_kernel,
        grid=(M // tm,),
        in_specs=[pl.BlockSpec((tm, 128), lambda i: (i, P_KR // 128)), tab, tab],
        out_specs=tab,
        out_shape=jax.ShapeDtypeStruct((M, 128), BF16),
        compiler_params=_cparams("parallel"),
        name="mla_k_rope",
    )(p, cos_t, sin_t)


def _merge_kernel(o_ref, hs_ref, dn_ref, wo_ref, wv_ref, wg_ref, wd_ref,
                  g0_ref, g1_ref, g2_ref, out_ref):
    hs = hs_ref[...]
    y_mla = jnp.dot(o_ref[...], wo_ref[...], preferred_element_type=F32)
    val = jnp.dot(hs, wv_ref[...], preferred_element_type=F32)
    gate = jnp.dot(hs, wg_ref[...], preferred_element_type=F32)
    y_dn = jnp.dot(dn_ref[...], wd_ref[...], preferred_element_type=F32)
    merged = (g0_ref[...].astype(F32) * y_mla
              + g1_ref[...].astype(F32) * (val * jax.nn.sigmoid(gate))
              + g2_ref[...].astype(F32) * y_dn)
    out_ref[...] = merged.astype(out_ref.dtype)


def merge_branches(o_mla, h_s5, o_dn, w_o, w_glu, w_dn_o, gates, *, tm, tn):
    M = o_mla.shape[0]
    D = w_o.shape[1]
    nj = D // tn

    def rows(width):
        return pl.BlockSpec((tm, width), lambda i, j: (i, 0))

    def cols(kdim, off):
        return pl.BlockSpec((kdim, tn), lambda i, j: (0, off * nj + j))

    def gate(b):
        return pl.BlockSpec((tm, tn), lambda i, j: (i, b * nj + j))

    return pl.pallas_call(
        _merge_kernel,
        grid=(M // tm, nj),
        in_specs=[rows(MLA_WIDTH), rows(S5_WIDTH), rows(DN_WIDTH),
                  cols(MLA_WIDTH, 0), cols(S5_WIDTH, 0), cols(S5_WIDTH, 1), cols(DN_WIDTH, 0),
                  gate(0), gate(1), gate(2)],
        out_specs=pl.BlockSpec((tm, tn), lambda i, j: (i, j)),
        out_shape=jax.ShapeDtypeStruct((M, D), BF16),
        compiler_params=_cparams("parallel", "parallel"),
        name="merge_branches",
    )(o_mla, h_s5, o_dn, w_o, w_glu, w_glu, w_dn_o, gates, gates, gates)


_NEG = -1e30


def _chunk_id(pos):
    return (pos + (CHUNK - N_META)) >> 6


ATTN_HEADS_PER_STEP = 4


def _attn_kernel(q_ref, kn_ref, kr_ref, v_ref, o_ref, *, tq, seq_pad):
    G = ATTN_HEADS_PER_STEP
    tk = tq
    tkm = tq + 128
    q0 = pl.program_id(2) * tq
    qs = [q_ref[:, g * MLA_QPAD:(g + 1) * MLA_QPAD] for g in range(G)]
    full_end = (_chunk_id(q0) + 1) * CHUNK - (CHUNK - N_META)
    n_full = full_end // tk

    def step(ks, width, carry, mask):
        ms, accs = carry
        k_rope = kr_ref[pl.ds(ks, width), :]
        ones = jnp.ones((width, 128), BF16)

        def scores(g):
            k = jnp.concatenate([kn_ref[pl.ds(ks, width), g * 128:(g + 1) * 128], k_rope], axis=1)
            s = lax.dot_general(qs[g], k, (((1,), (1,)), ((), ())), preferred_element_type=F32)
            return s if mask is None else jnp.where(mask, s, _NEG)

        def weights(g, s):
            m_new = jnp.maximum(ms[g], jnp.max(s, axis=-1, keepdims=True))
            return m_new, jnp.exp2(s - m_new).astype(BF16), jnp.exp2(ms[g] - m_new)

        def update(g, p, alpha):
            v1 = jnp.concatenate([v_ref[pl.ds(ks, width), g * 128:(g + 1) * 128], ones], axis=1)
            return alpha * accs[g] + jnp.dot(p, v1, preferred_element_type=F32)

        ss = [scores(g) for g in range(G)]
        m_new, ps, alphas = zip(*[weights(g, ss[g]) for g in range(G)])
        return list(m_new), [update(g, ps[g], alphas[g]) for g in range(G)]

    def full_steps(lo, hi, width, carry):
        return lax.fori_loop(
            lo, hi, lambda j, c: step(pl.multiple_of(j * width, width), width, c, None), carry)

    init = ([jnp.full((tq, 1), _NEG, F32)] * G, [jnp.zeros((tq, 2 * MLA_V), F32)] * G)
    carry = full_steps(0, n_full // 4, 4 * tk, init)
    carry = full_steps((n_full // 4) * 2, n_full // 2, 2 * tk, carry)
    carry = full_steps(n_full - n_full % 2, n_full, tk, carry)

    first = n_full * tk
    ks = pl.multiple_of(jnp.minimum(first, seq_pad - tkm), 128)
    kpos = ks + lax.broadcasted_iota(jnp.int32, (1, tkm), 1)
    cq = _chunk_id(q0 + lax.broadcasted_iota(jnp.int32, (tq, 1), 0))
    mask = jnp.logical_and(cq >= _chunk_id(kpos), kpos >= first)
    _, accs = step(ks, tkm, carry, mask)
    for g in range(G):
        o_ref[:, g * MLA_V:(g + 1) * MLA_V] = (
            accs[g][:, :MLA_V] / accs[g][:, MLA_V:]).astype(o_ref.dtype)


def mla_attention(q, kv, kr, *, batch, seq_pad, tq):
    M = q.shape[0]
    nq = seq_pad // tq
    G = ATTN_HEADS_PER_STEP
    ng = MLA_HEADS // G
    assert tq + 128 <= seq_pad and tq > CHUNK
    return pl.pallas_call(
        functools.partial(_attn_kernel, tq=tq, seq_pad=seq_pad),
        grid=(batch, ng, nq),
        in_specs=[pl.BlockSpec((tq, G * MLA_QPAD), lambda b, h, i: (b * nq + i, h)),
                  pl.BlockSpec((seq_pad, G * 128), lambda b, h, i: (b, h)),
                  pl.BlockSpec((seq_pad, 128), lambda b, h, i: (b, 0)),
                  pl.BlockSpec((seq_pad, G * 128), lambda b, h, i: (b, ng + h))],
        out_specs=pl.BlockSpec((tq, G * MLA_V), lambda b, h, i: (b * nq + i, h)),
        out_shape=jax.ShapeDtypeStruct((M, MLA_WIDTH), BF16),
        compiler_params=_cparams("parallel", "parallel", "parallel"),
        name="mla_attention",
    )(q, kv, kr, kv)


def _s5_kernel(u_ref, br_ref, bi_ref, cr_ref, ci_ref, d_ref, ast_ref, pw_ref, o_ref,
               xr_s, xi_s, car_s, *, tt):
    n = S5_GPB * S5_STATE

    @pl.when(pl.program_id(2) == 0)
    def _():
        car_s[...] = jnp.zeros_like(car_s)

    u = u_ref[...]
    ub = u.astype(BF16)
    xr = jnp.dot(ub, br_ref[...], preferred_element_type=F32).reshape(tt // 8, 8, n)
    xi = jnp.dot(ub, bi_ref[...], preferred_element_type=F32).reshape(tt // 8, 8, n)
    for si in range(3):
        ar = ast_ref[2 * si]
        ai = ast_ref[2 * si + 1]
        sr = pltpu.roll(xr, 1 << si, axis=1)
        sm = pltpu.roll(xi, 1 << si, axis=1)
        xr, xi = xr + (ar * sr - ai * sm), xi + (ar * sm + ai * sr)
    xr_s[...] = xr.reshape(tt, n)
    xi_s[...] = xi.reshape(tt, n)
    pr = pw_ref[0]
    pi = pw_ref[1]

    def body(i, carry):
        cr, ci = carry
        sl = pl.ds(pl.multiple_of(i * 8, 8), 8)
        a = xr_s[sl, :] + (pr * cr - pi * ci)
        b = xi_s[sl, :] + (pr * ci + pi * cr)
        xr_s[sl, :] = a
        xi_s[sl, :] = b
        return (jnp.broadcast_to(a[7:8, :], (8, n)), jnp.broadcast_to(b[7:8, :], (8, n)))

    cr, ci = lax.fori_loop(0, tt // 8, body, (car_s[0], car_s[1]))
    car_s[0] = cr
    car_s[1] = ci
    y = (jnp.dot(xr_s[...].astype(BF16), cr_ref[...], preferred_element_type=F32)
         - jnp.dot(xi_s[...].astype(BF16), ci_ref[...], preferred_element_type=F32)
         + d_ref[...] * u)
    o_ref[...] = jax.nn.gelu(y).astype(o_ref.dtype)


def s5_scan(p, s5c, *, batch, seq_pad, tt):
    M = p.shape[0]
    nt = seq_pad // tt
    cw = S5_GPB * S5_GROUP
    n = S5_GPB * S5_STATE
    ublk = P_S5 // cw
    return pl.pallas_call(
        functools.partial(_s5_kernel, tt=tt),
        grid=(batch, S5_GB, nt),
        in_specs=[pl.BlockSpec((tt, cw), lambda b, g, t: (b * nt + t, ublk + g)),
                  pl.BlockSpec((None, cw, n), lambda b, g, t: (g, 0, 0)),
                  pl.BlockSpec((None, cw, n), lambda b, g, t: (g, 0, 0)),
                  pl.BlockSpec((None, n, cw), lambda b, g, t: (g, 0, 0)),
                  pl.BlockSpec((None, n, cw), lambda b, g, t: (g, 0, 0)),
                  pl.BlockSpec((None, 1, cw), lambda b, g, t: (g, 0, 0)),
                  pl.BlockSpec((None, 6, 8, n), lambda b, g, t: (g, 0, 0, 0)),
                  pl.BlockSpec((None, 2, 8, n), lambda b, g, t: (g, 0, 0, 0))],
        out_specs=pl.BlockSpec((tt, cw), lambda b, g, t: (b * nt + t, g)),
        out_shape=jax.ShapeDtypeStruct((M, S5_WIDTH), BF16),
        scratch_shapes=[pltpu.VMEM((tt, n), F32), pltpu.VMEM((tt, n), F32),
                        pltpu.VMEM((2, 8, n), F32)],
        compiler_params=_cparams("parallel", "parallel", "arbitrary"),
        name="s5_scan",
    )(p, s5c["br"], s5c["bi"], s5c["cr"], s5c["ci"], s5c["d"], s5c["ast"], s5c["pw"])


def s5_constants(a_re, a_im, log_dt, b_re, b_im, c_re, c_im, d):
    ar, ai = a_re.astype(F32), a_im.astype(F32)
    delta = jnp.exp(log_dt.astype(F32))[:, None]
    mag = jnp.exp(ar * delta)
    abar_r, abar_i = mag * jnp.cos(ai * delta), mag * jnp.sin(ai * delta)
    den = ar * ar + ai * ai
    zr = ((abar_r - 1.0) * ar + abar_i * ai) / den
    zi = (abar_i * ar - (abar_r - 1.0) * ai) / den
    br, bi = b_re.astype(F32), b_im.astype(F32)
    bbar_r = zr[..., None] * br - zi[..., None] * bi
    bbar_i = zr[..., None] * bi + zi[..., None] * br

    eye = jnp.eye(S5_GPB, dtype=F32)

    def in_blockdiag(m):
        m = m.reshape(S5_GB, S5_GPB, S5_STATE, S5_GROUP)
        bd = jnp.einsum("bgpc,gh->bgchp", m, eye)
        return bd.reshape(S5_GB, S5_GPB * S5_GROUP, S5_GPB * S5_STATE).astype(BF16)

    def out_blockdiag(m):
        m = m.reshape(S5_GB, S5_GPB, S5_GROUP, S5_STATE)
        bd = jnp.einsum("bgcp,gh->bgphc", m, eye)
        return bd.reshape(S5_GB, S5_GPB * S5_STATE, S5_GPB * S5_GROUP).astype(BF16)

    def cmul(x, y):
        return x[0] * y[0] - x[1] * y[1], x[0] * y[1] + x[1] * y[0]

    n = S5_GPB * S5_STATE
    a1 = (abar_r.reshape(S5_GB, n), abar_i.reshape(S5_GB, n))
    a2 = cmul(a1, a1)
    a4 = cmul(a2, a2)
    row = jnp.arange(8)[None, :, None]
    ast = []
    for s, a in ((1, a1), (2, a2), (4, a4)):
        for part in a:
            ast.append(jnp.where(row >= s, part[:, None, :], 0.0))
    ast = jnp.stack(ast, axis=1)
    pows = [a1]
    for _ in range(7):
        pows.append(cmul(pows[-1], a1))
    pw = jnp.stack([jnp.stack([p[0] for p in pows], axis=1),
                    jnp.stack([p[1] for p in pows], axis=1)], axis=1)
    return {
        "br": in_blockdiag(bbar_r), "bi": in_blockdiag(bbar_i),
        "cr": out_blockdiag(c_re.astype(F32)), "ci": out_blockdiag(c_im.astype(F32)),
        "d": d.astype(F32).reshape(S5_GB, 1, S5_GPB * S5_GROUP),
        "ast": ast, "pw": pw,
    }


def _dn_gates_kernel(ab_ref, alog_ref, dtb_ref, gcum_ref, beta_ref):
    ab = ab_ref[...]
    g = -jnp.exp(alog_ref[...]) * jax.nn.softplus(ab[:DN_HEADS] + dtb_ref[...])
    beta_ref[...] = jax.nn.sigmoid(ab[DN_HEADS:])
    lane = lax.broadcasted_iota(jnp.int32, g.shape, 1) % CHUNK
    s = 1
    while s < CHUNK:
        g = g + jnp.where(lane >= s, pltpu.roll(g, s, axis=1), 0.0)
        s *= 2
    gcum_ref[...] = g


def dn_gates(ab_t, a_log, dt_bias):
    B, _, Lp = ab_t.shape
    col = pl.BlockSpec((DN_HEADS, 1), lambda b: (0, 0))
    out = pl.BlockSpec((None, DN_HEADS, Lp), lambda b: (b, 0, 0))
    return pl.pallas_call(
        _dn_gates_kernel,
        grid=(B,),
        in_specs=[pl.BlockSpec((None, 2 * DN_HEADS, Lp), lambda b: (b, 0, 0)), col, col],
        out_specs=[out, out],
        out_shape=[jax.ShapeDtypeStruct((B, DN_HEADS, Lp), F32)] * 2,
        compiler_params=_cparams("parallel"),
        name="dn_gates",
    )(ab_t, a_log.astype(F32).reshape(DN_HEADS, 1), dt_bias.astype(F32).reshape(DN_HEADS, 1))


def _dot(a, b):
    return jnp.dot(a, b, preferred_element_type=F32)


def _dot_nt(a, b):
    return lax.dot_general(a, b, (((1,), (1,)), ((), ())), preferred_element_type=F32)


def _dot_tn(a, b):
    return lax.dot_general(a, b, (((0,), (0,)), ((), ())), preferred_element_type=F32)


def _l2(x):
    return x * lax.rsqrt(jnp.sum(x * x, axis=-1, keepdims=True) + EPS)


def _dn_prep_kernel(q_ref, k_ref, v_ref, cwq_ref, cwk_ref, cwv_ref, bcol_ref, gcol_ref, grow_ref,
                    u_ref, w_ref, qg_ref, kd_ref, qk_ref, dl_ref, *, n_chunks, unroll):
    C = CHUNK

    def conv_silu(x_ref, cw_ref, c, sl):
        cur = x_ref[sl, :]
        halo = x_ref[pl.ds(pl.multiple_of(jnp.maximum(c * C - 8, 0), 8), 8), :]
        xc = jnp.concatenate([jnp.where(c > 0, halo, 0.0), cur], axis=0)
        acc = cur * cw_ref[DN_CONV - 1:DN_CONV, :]
        for j in range(DN_CONV - 1):
            lo = 8 - (DN_CONV - 1) + j
            acc = acc + xc[lo:lo + C, :] * cw_ref[j:j + 1, :]
        return jax.nn.silu(acc)

    ii = lax.broadcasted_iota(jnp.int32, (C, C), 0)
    jj = lax.broadcasted_iota(jnp.int32, (C, C), 1)
    tri = ii >= jj
    strict = ii > jj
    eye = (ii == jj).astype(F32)

    def body(i, carry):
        U = range(unroll)
        cs = [i * unroll + j for j in U]
        sls = [pl.ds(pl.multiple_of(c * C, C), C) for c in cs]
        q = [_l2(conv_silu(q_ref, cwq_ref, cs[j], sls[j])) * (DN_DK ** -0.5) for j in U]
        k = [_l2(conv_silu(k_ref, cwk_ref, cs[j], sls[j])) for j in U]
        v = [conv_silu(v_ref, cwv_ref, cs[j], sls[j]) for j in U]
        beta = [bcol_ref[sl, :] for sl in sls]
        gc = [gcol_ref[sl, :] for sl in sls]
        decay = [jnp.exp(jnp.where(tri, gc[j] - grow_ref[pl.ds(cs[j], 1), :], -jnp.inf)) for j in U]
        kb = [k[j] * beta[j] for j in U]
        a_mat = [jnp.where(strict, _dot_nt(kb[j], k[j]) * decay[j], 0.0) for j in U]
        t_inv = [eye - a for a in a_mat]
        pw = a_mat
        for _ in range(5):
            pw = [_dot(x, x) for x in pw]
            t_inv = [t + _dot(t, x) for t, x in zip(t_inv, pw)]
        u = [_dot(t_inv[j], v[j] * beta[j]) for j in U]
        w = [_dot(t_inv[j], kb[j] * jnp.exp(gc[j])) for j in U]
        qk = [_dot_nt(q[j].astype(BF16), k[j].astype(BF16)) * decay[j] for j in U]
        for j in U:
            sl = sls[j]
            u_ref[sl, :] = u[j]
            w_ref[sl, :] = w[j].astype(w_ref.dtype)
            qk_ref[sl, :] = qk[j].astype(qk_ref.dtype)
            qg_ref[sl, :] = (q[j] * jnp.exp(gc[j])).astype(qg_ref.dtype)
            g_last = gc[j][C - 1:C, :]
            kd_ref[sl, :] = (k[j] * jnp.exp(g_last - gc[j])).astype(kd_ref.dtype)
            dl_ref[pl.ds(cs[j], 1), :] = jnp.broadcast_to(jnp.exp(g_last), (1, dl_ref.shape[1]))
        return carry

    lax.fori_loop(0, n_chunks // unroll, body, 0)


def dn_prep(p, conv_w, beta_col, gcum_col, gcum_row, *, batch, seq_pad):
    M = p.shape[0]
    H = DN_HEADS
    nc = seq_pad // CHUNK
    c0 = P_DNQKV // 128

    def head(off):
        return pl.BlockSpec((seq_pad, 128), lambda b, h: (b, off + h))

    def taps(off):
        return pl.BlockSpec((DN_CONV, 128), lambda b, h: (0, off + h))

    colspec = pl.BlockSpec((None, None, seq_pad, 1), lambda b, h: (b, h, 0, 0))
    wide = jax.ShapeDtypeStruct((M, DN_WIDTH), BF16)
    conv_w = conv_w.astype(F32)
    return pl.pallas_call(
        functools.partial(_dn_prep_kernel, n_chunks=nc, unroll=_pick(nc, (11, 6, 4, 3, 2, 1))),
        grid=(batch, H),
        in_specs=[head(c0), head(c0 + H), head(c0 + 2 * H), taps(0), taps(H), taps(2 * H),
                  colspec, colspec,
                  pl.BlockSpec((None, None, nc, CHUNK), lambda b, h: (b, h, 0, 0))],
        out_specs=[head(0), head(0), head(0), head(0),
                   pl.BlockSpec((None, None, seq_pad, CHUNK), lambda b, h: (b, h, 0, 0)),
                   pl.BlockSpec((None, None, nc, 128), lambda b, h: (b, h, 0, 0))],
        out_shape=[jax.ShapeDtypeStruct((M, DN_WIDTH), F32), wide, wide, wide,
                   jax.ShapeDtypeStruct((batch, H, seq_pad, CHUNK), BF16),
                   jax.ShapeDtypeStruct((batch, H, nc, 128), F32)],
        compiler_params=_cparams("parallel", "parallel"),
        name="dn_prep",
    )(p, p, p, conv_w, conv_w, conv_w, beta_col, gcum_col, gcum_row)


def _dn_scan_kernel(u_ref, w_ref, qg_ref, kd_ref, qk_ref, dl_ref, z_ref, gn_ref, o_ref, s_ref,
                    *, chunks_per_tile):
    C = CHUNK

    @pl.when(pl.program_id(1) == 0)
    def _():
        s_ref[...] = jnp.zeros_like(s_ref)

    gn = gn_ref[...]

    def body(c, carry):
        sl = pl.ds(pl.multiple_of(c * C, C), C)
        H = range(DN_HEADS)
        hs = [slice(j * DN_DV, (j + 1) * DN_DV) for j in H]
        S = [s_ref[j] for j in H]
        Sb = [s.astype(BF16) for s in S]
        v_new = [u_ref[sl, hs[j]] - _dot(w_ref[sl, hs[j]], Sb[j]) for j in H]
        o_inter = [_dot(qg_ref[sl, hs[j]], Sb[j]) for j in H]
        vb = [v.astype(BF16) for v in v_new]
        o = [o_inter[j] + _dot(qk_ref[j, sl, :], vb[j]) for j in H]
        s_upd = [_dot_tn(kd_ref[sl, hs[j]], vb[j]) for j in H]
        for j in H:
            s_ref[j] = S[j] * dl_ref[j, pl.ds(c, 1), :] + s_upd[j]
            o_ref[sl, hs[j]] = (_rms(o[j], gn) * jax.nn.silu(z_ref[sl, hs[j]])).astype(o_ref.dtype)
        return carry

    lax.fori_loop(0, chunks_per_tile, body, 0)


def dn_scan(u, w, qg, kd, qk, dl, p, out_norm_g, *, batch, seq_pad):
    M = p.shape[0]
    H = DN_HEADS
    nc = seq_pad // CHUNK
    cpt = _pick(nc, (11, 6, 4, 3, 2, 1))
    nt = nc // cpt
    tt = cpt * CHUNK
    dl = dl.reshape(batch, H, nt, cpt, 128)

    def rows(col_block=0):
        return pl.BlockSpec((tt, DN_WIDTH), lambda b, t: (b * nt + t, col_block))

    return pl.pallas_call(
        functools.partial(_dn_scan_kernel, chunks_per_tile=cpt),
        grid=(batch, nt),
        in_specs=[rows(), rows(), rows(), rows(),
                  pl.BlockSpec((None, H, tt, CHUNK), lambda b, t: (b, 0, t, 0)),
                  pl.BlockSpec((None, H, None, cpt, 128), lambda b, t: (b, 0, t, 0, 0)),
                  rows(P_DNZ // DN_WIDTH),
                  pl.BlockSpec((1, DN_DV), lambda b, t: (0, 0))],
        out_specs=rows(),
        out_shape=jax.ShapeDtypeStruct((M, DN_WIDTH), BF16),
        scratch_shapes=[pltpu.VMEM((H, DN_DK, DN_DV), F32)],
        compiler_params=_cparams("parallel", "arbitrary"),
        name="dn_scan",
    )(u, w, qg, kd, qk, dl, p, out_norm_g.astype(F32).reshape(1, DN_DV))


def _rot_cols(w):
    half = w.shape[-1] // 2
    return jnp.concatenate([-w[..., half:], w[..., :half]], axis=-1)


def _regroup_in_proj(wt):
    kr = wt[OFF_KR:OFF_S5]
    kr_rot = jnp.concatenate([-kr[MLA_ROPE // 2:], kr[:MLA_ROPE // 2]], axis=0)
    return jnp.concatenate([
        wt[OFF_Q:OFF_KV], wt[OFF_DN_Z:OFF_DN_A], wt[OFF_S5:OFF_DN_Z], wt[OFF_KV:OFF_KR],
        kr, kr_rot, wt[OFF_DN_A:OFF_GATE],
        jnp.zeros((128 - 2 * DN_HEADS, D_MODEL), BF16)], axis=0)


def _prep_mla(mla_w_uq, mla_w_ukv):
    depth = mla_w_uq.shape[0]
    uq = mla_w_uq.reshape(depth, MLA_Q_LORA, MLA_HEADS, MLA_QK)
    w_uq = jnp.concatenate([uq, _rot_cols(uq[..., MLA_NOPE:])], axis=-1)
    w_uq = w_uq.reshape(depth, MLA_Q_LORA, MLA_HEADS * MLA_QPAD).astype(BF16)
    ukv = mla_w_ukv.reshape(depth, MLA_KV_LORA, MLA_HEADS, 2, 128)
    w_ukv = jnp.transpose(ukv, (0, 1, 3, 2, 4)).reshape(depth, MLA_KV_LORA, 2 * MLA_HEADS * 128)
    return w_uq, w_ukv.astype(BF16)


def _rope_tables(seq_pad, batch):
    inv = ROPE_THETA ** (-jnp.arange(0, MLA_ROPE, 2, dtype=F32) / MLA_ROPE)
    ang = jnp.arange(seq_pad, dtype=F32)[:, None] * inv[None, :]
    zeros = jnp.zeros((seq_pad, 128 - MLA_ROPE), F32)
    cos_t = jnp.concatenate([jnp.cos(ang), jnp.cos(ang), zeros], axis=1)
    sin_t = jnp.concatenate([jnp.sin(ang), jnp.sin(ang), zeros], axis=1)
    return jnp.tile(cos_t, (batch, 1)), jnp.tile(sin_t, (batch, 1))


def _ffn(xn, w13, w2, layer, extra_casts, *, tm_up, tm):
    hid, (w2_bf16, *extra) = matmul_swiglu(xn, w13, layer, [w2] + extra_casts, tm=tm_up, tn=256)
    return matmul(hid, w2_bf16, tm=tm // 2, tn=512, out_dtype=BF16, name="ffn_down"), extra


def kernel(x, meta_tokens, sandwich_g, ffn1_w13, ffn1_w2, w_in, mla_q_norm_g, mla_kv_norm_g, mla_w_uq, mla_w_ukv, mla_w_o, s5_a_re, s5_a_im, s5_log_dt, s5_b_re, s5_b_im, s5_c_re, s5_c_im, s5_d, s5_w_glu, dn_conv_w, dn_a_log, dn_dt_bias, dn_out_norm_g, dn_w_o, w_out, ffn2_w13, ffn2_w2):
    B, seq, D = x.shape
    depth = w_in.shape[0]
    L = N_META + seq
    assert (L - N_META) % CHUNK == 0, "pad keys are hidden by the chunk mask only on a chunk boundary"
    Lp = -(-L // 128) * 128
    M = B * Lp
    tm = _pick(M, (1056, 768, 384, 128))
    tm_up = _pick(M, (2112, 1536, 768, 384, 128))
    tr = _pick(M, (176, 128))
    tt = _pick(Lp, (528, 384, 128))
    ta = _pick(Lp, tuple(c for c in (384, 128) if c + 128 <= Lp))

    meta = jnp.broadcast_to(meta_tokens[None].astype(x.dtype), (B, N_META, D))
    h = jnp.concatenate([meta, x, jnp.zeros((B, Lp - L, D), x.dtype)], axis=1).reshape(M, D)
    cos_t, sin_t = _rope_tables(Lp, B)

    w_in_t = jnp.swapaxes(w_in, 1, 2)
    w_uq, w_ukv = _prep_mla(mla_w_uq, mla_w_ukv)
    q_scale = MLA_QK ** -0.5 * math.log2(math.e)

    xn = rmsnorm_cast(h, sandwich_g[0, 0], col_block=0, width=D, tm=tr)
    for l in range(depth):
        g = sandwich_g[l]
        y, (w_in_b, w_mla_o, w_glu, w_dn_o, w_mix) = _ffn(
            xn, ffn1_w13, ffn1_w2, l, [w_in_t, mla_w_o, s5_w_glu, dn_w_o, w_out], tm_up=tm_up, tm=tm)
        h, xn = resid_norm(h, y, g[1], g[2], coef=0.5, tm=tr)

        p = matmul(xn, _regroup_in_proj(w_in_b), tm=tm, tn=768, out_dtype=F32, w_rows=(0, P_WIDTH),
                   name="in_proj_small")
        gates = matmul(xn, w_in_b, tm=tm, tn=1024, out_dtype=BF16, act="sigmoid",
                       w_rows=(OFF_GATE, 3 * D), name="in_proj_gates")

        qn = rmsnorm_cast(p, mla_q_norm_g[l], col_block=P_Q // MLA_Q_LORA, width=MLA_Q_LORA, tm=tm)
        kvn = rmsnorm_cast(p, mla_kv_norm_g[l], col_block=P_KV // MLA_KV_LORA, width=MLA_KV_LORA, tm=tm)
        q = matmul_qrope(qn, w_uq, l, cos_t, sin_t, tm=tm, heads_per_tile=4, scale=q_scale)
        kv = matmul(kvn, w_ukv, l, tm=tm, tn=1024, out_dtype=BF16, name="mla_kv_proj")
        kr = k_rope(p, cos_t, sin_t, tm=tm)
        o_mla = mla_attention(q, kv, kr, batch=B, seq_pad=Lp, tq=ta)

        s5c = s5_constants(s5_a_re[l], s5_a_im[l], s5_log_dt[l], s5_b_re[l], s5_b_im[l],
                           s5_c_re[l], s5_c_im[l], s5_d[l])
        h_s5 = s5_scan(p, s5c, batch=B, seq_pad=Lp, tt=tt)

        ab_t = jnp.transpose(p[:, P_AB:P_AB + 2 * DN_HEADS].reshape(B, Lp, 2 * DN_HEADS), (0, 2, 1))
        gcum, beta = dn_gates(ab_t, dn_a_log[l], dn_dt_bias[l])
        u, w, qg, kd, qk, dl = dn_prep(p, dn_conv_w[l], beta[..., None], gcum[..., None],
                                       gcum.reshape(B, DN_HEADS, Lp // CHUNK, CHUNK),
                                       batch=B, seq_pad=Lp)
        o_dn = dn_scan(u, w, qg, kd, qk, dl, p, dn_out_norm_g[l], batch=B, seq_pad=Lp)

        merged = merge_branches(o_mla, h_s5, o_dn, w_mla_o, w_glu, w_dn_o, gates, tm=tm, tn=512)
        mix = matmul(merged, w_mix, tm=tm, tn=1024, out_dtype=BF16, name="out_proj")
        h, xn = resid_norm(h, mix, g[3], g[4], coef=1.0, tm=tr)

        y, _ = _ffn(xn, ffn2_w13, ffn2_w2, l, [], tm_up=tm_up, tm=tm)
        if l + 1 < depth:
            h, xn = resid_norm(h, y, g[5], sandwich_g[l + 1, 0], coef=0.5, tm=tr)
    out = resid_out(h, y, g[5], coef=0.5, batch=B, seq_pad=Lp, first=N_META, count=seq,
                    tm=_pick(seq, (256, 128, 64)))
    return out.reshape(B, seq, D)
```
